```python
import math
import jax, jax.numpy as jnp
from jax import lax
import numpy as np

D_MODEL = 1024
BATCH = 4
SEQ = 8192
DEPTH = 2

GRID_W = 64
CTX_LEN = 256
N_EVEN = (DEPTH + 1) // 2
N_ODD = DEPTH // 2
N_MOD = 6
EPS = 1e-6

RNN_W = 3 * D_MODEL // 4
RNN_HD = 64
RNN_HEADS = RNN_W // RNN_HD
RNN_CONV = 4
LRU_C = 8.0
POOL_W = D_MODEL // 4
POOL_WINDOWS = (2, 4, 8, 16)
POOL_GROUPS = len(POOL_WINDOWS)
POOL_GD = POOL_W // POOL_GROUPS
ATT_HD = 64
ATT_VD = 2 * ATT_HD
ATT_HEADS = (3 * D_MODEL // 4) // ATT_VD
ATT_W = ATT_HEADS * 2 * ATT_HD
Q_BLOCK = 128
ROPE_BASE = 10000.0
ROPE_AX = ATT_HD // 4
CONF_W = D_MODEL // 4
CONF_K = 31
D_FF = 2816
FFN_K = 3

AB_IN = 2 * RNN_W + POOL_W
AB_OUT = RNN_W + POOL_W
CD_IN = 3 * ATT_W + 2 * CONF_W
CD_OUT = ATT_HEADS * ATT_VD + CONF_W

kernel_name = 'hybrid_lru_pool_diffattn_conformer_dit'


def rmsnorm(x, g):
    xf = x.astype(jnp.float32)
    y = xf * lax.rsqrt(jnp.mean(xf * xf, axis=-1, keepdims=True) + EPS)
    return (y * g.astype(jnp.float32)).astype(x.dtype)


def layernorm(x, g, b):
    xf = x.astype(jnp.float32)
    mu = jnp.mean(xf, axis=-1, keepdims=True)
    var = jnp.mean(jnp.square(xf - mu), axis=-1, keepdims=True)
    y = (xf - mu) * lax.rsqrt(var + EPS)
    return (y * g.astype(jnp.float32) + b.astype(jnp.float32)).astype(x.dtype)


def modulate(x, g, shift, scale):
    return rmsnorm(x, g) * (1.0 + scale) + shift


def adaln(cvec, w, b):
    m = jax.nn.silu(cvec) @ w + b
    return jnp.split(m, N_MOD, axis=-1)


def dwconv(x, w, b, pad_lo, pad_hi):
    y = lax.conv_general_dilated(x, w[:, None, :].astype(x.dtype), window_strides=(1,),
                                 padding=[(pad_lo, pad_hi)], dimension_numbers=('NWC', 'WIO', 'NWC'),
                                 feature_group_count=x.shape[-1])
    return y + b


def centred_mean(x, w):
    L = x.shape[1]
    xf = x.astype(jnp.float32)
    S = jnp.concatenate([jnp.zeros_like(xf[:, :1]), jnp.cumsum(xf, axis=1)], axis=1)
    t = jnp.arange(L)
    lo = jnp.clip(t - w // 2, 0, L)
    hi = jnp.clip(t + w - w // 2, 0, L)
    cnt = (hi - lo).astype(jnp.float32)
    return ((S[:, hi] - S[:, lo]) / cnt[None, :, None]).astype(x.dtype)


def pool_mixer(xb, w_pool, scale):
    B, L, _ = xb.shape
    xg = xb.reshape(B, L, POOL_GROUPS, POOL_GD)
    d = jnp.stack([centred_mean(xg[:, :, g], POOL_WINDOWS[g]) - xg[:, :, g] for g in range(POOL_GROUPS)], axis=2)
    y = jnp.einsum('blgc,gcd->blgd', d, w_pool)
    return y.reshape(B, L, POOL_W) * scale


def lru_coeffs(xc, wa, ba, wx, bx, lam):
    B, L, _ = xc.shape
    xh = xc.reshape(B, L, RNN_HEADS, RNN_HD)
    r = jax.nn.sigmoid(jnp.einsum('blhi,hij->blhj', xh, wa).reshape(B, L, RNN_W) + ba)
    i = jax.nn.sigmoid(jnp.einsum('blhi,hij->blhj', xh, wx).reshape(B, L, RNN_W) + bx)
    log_a = -LRU_C * r.astype(jnp.float32) * jax.nn.softplus(-lam.astype(jnp.float32))
    a = jnp.exp(log_a)
    b = jnp.sqrt(-jnp.expm1(2.0 * log_a)) * (i * xc).astype(jnp.float32)
    return a, b


def _combine(e1, e2):
    a1, b1 = e1
    a2, b2 = e2
    return a1 * a2, a2 * b1 + b2


def linear_scan(a, b, reverse, h0=None):
    A, H = lax.associative_scan(_combine, (a, b), reverse=reverse, axis=1)
    if h0 is None:
        return H
    return H + A * h0[:, None, :]


def mix_ab(h, hc, w_in, w_out, conv_w, conv_b, wa, ba, wx, bx, lam, pool_w, pool_scale, need_ctx):
    pad_lo, pad_hi = RNN_CONV // 2, RNN_CONV - 1 - RNN_CONV // 2
    z = h @ w_in
    xa = dwconv(z[..., :RNN_W], conv_w, conv_b, pad_lo, pad_hi)
    ga = z[..., RNN_W:2 * RNN_W]
    xb = z[..., 2 * RNN_W:]
    xa_c = dwconv(hc @ w_in[:, :RNN_W], conv_w, conv_b, pad_lo, pad_hi)
    af, bf = lru_coeffs(xa_c, wa[0], ba[0], wx[0], bx[0], lam[0])
    ab, bb = lru_coeffs(xa_c, wa[1], ba[1], wx[1], bx[1], lam[1])
    hf_c = linear_scan(af, bf, False)
    hb_c = linear_scan(ab, bb, True)
    af, bf = lru_coeffs(xa, wa[0], ba[0], wx[0], bx[0], lam[0])
    ab, bb = lru_coeffs(xa, wa[1], ba[1], wx[1], bx[1], lam[1])
    hf = linear_scan(af, bf, False, hf_c[:, -1])
    hb = linear_scan(ab, bb, True, hb_c[:, 0])
    y_a = (hf + hb).astype(h.dtype) * jax.nn.gelu(ga)
    y_b = pool_mixer(xb, pool_w, pool_scale)
    y = jnp.concatenate([y_a, y_b], axis=-1) @ w_out
    if not need_ctx:
        return y, None
    zc = hc @ w_in[:, RNN_W:]
    yc_a = (hf_c + hb_c).astype(hc.dtype) * jax.nn.gelu(zc[..., :RNN_W])
    yc_b = pool_mixer(zc[..., RNN_W:], pool_w, pool_scale)
    yc = jnp.concatenate([yc_a, yc_b], axis=-1) @ w_out
    return y, yc


def _rot(x, cos, sin):
    x1, x2 = x[..., :ROPE_AX], x[..., ROPE_AX:]
    return jnp.concatenate([x1 * cos - x2 * sin, x1 * sin + x2 * cos], axis=-1)


def apply_rope(x, rope):
    cr, sr, cc, sc = [r[None, :, None, None, :] for r in rope]
    y = jnp.concatenate([_rot(x[..., :2 * ROPE_AX], cr, sr), _rot(x[..., 2 * ROPE_AX:], cc, sc)], axis=-1)
    return y.astype(x.dtype)


def diff_attend(q, k, v, lam):
    s = jnp.einsum('bqhmd,bkhmd->bhmqk', q, k, preferred_element_type=jnp.float32) * (ATT_HD ** -0.5)
    p = jax.nn.softmax(s, axis=-1)
    w = p[:, :, 0] - lam * p[:, :, 1]
    return jnp.einsum('bhqk,bkhe->bqhe', w.astype(v.dtype), v)


def conformer_conv(u, dw_w, dw_b, ln_g, ln_b):
    g = u[..., :CONF_W] * jax.nn.sigmoid(u[..., CONF_W:])
    g = dwconv(g, dw_w, dw_b, CONF_K // 2, CONF_K // 2)
    return jax.nn.silu(layernorm(g, ln_g, ln_b))


def mix_cd(h, hc, w_in, w_out, lq1, lk1, lq2, lk2, subln_g, dw_w, dw_b, ln_g, ln_b, rope, lambda_init, need_ctx):
    B, L, _ = h.shape
    Lc = hc.shape[1]
    z = h @ w_in
    q = apply_rope(z[..., :ATT_W].reshape(B, L, ATT_HEADS, 2, ATT_HD), rope)
    k = apply_rope(z[..., ATT_W:2 * ATT_W].reshape(B, L, ATT_HEADS, 2, ATT_HD), rope)
    v = z[..., 2 * ATT_W:3 * ATT_W].reshape(B, L, ATT_HEADS, ATT_VD)
    u = z[..., 3 * ATT_W:]
    zkv = hc @ w_in[:, ATT_W:3 * ATT_W]
    kc = zkv[..., :ATT_W].reshape(B, Lc, ATT_HEADS, 2, ATT_HD)
    vc = zkv[..., ATT_W:].reshape(B, Lc, ATT_HEADS, ATT_VD)
    lam = (jnp.exp(jnp.sum(lq1.astype(jnp.float32) * lk1.astype(jnp.float32)))
           - jnp.exp(jnp.sum(lq2.astype(jnp.float32) * lk2.astype(jnp.float32))) + lambda_init)
    k_all = jnp.concatenate([k, kc], axis=1)
    v_all = jnp.concatenate([v, vc], axis=1)
    nb = L // Q_BLOCK
    qb = jnp.moveaxis(q.reshape(B, nb, Q_BLOCK, ATT_HEADS, 2, ATT_HD), 1, 0)
    o = lax.map(lambda qq: diff_attend(qq, k_all, v_all, lam), qb)
    o = jnp.moveaxis(o, 0, 1).reshape(B, L, ATT_HEADS, ATT_VD)
    y_c = (rmsnorm(o, subln_g) * (1.0 - lambda_init)).reshape(B, L, ATT_HEADS * ATT_VD)
    y_d = conformer_conv(u, dw_w, dw_b, ln_g, ln_b)
    y = jnp.concatenate([y_c, y_d], axis=-1) @ w_out
    if not need_ctx:
        return y, None
    qc = (hc @ w_in[:, :ATT_W]).reshape(B, Lc, ATT_HEADS, 2, ATT_HD)
    uc = hc @ w_in[:, 3 * ATT_W:]
    oc = diff_attend(qc, kc, vc, lam)
    yc_c = (rmsnorm(oc, subln_g) * (1.0 - lambda_init)).reshape(B, Lc, ATT_HEADS * ATT_VD)
    yc_d = conformer_conv(uc, dw_w, dw_b, ln_g, ln_b)
    yc = jnp.concatenate([yc_c, yc_d], axis=-1) @ w_out
    return y, yc


def conv_ffn(h, w_up, conv_w, conv_b, w_down):
    u = dwconv(h @ w_up, conv_w, conv_b, FFN_K // 2, FFN_K // 2)
    return (jax.nn.silu(u[..., D_FF:]) * u[..., :D_FF]) @ w_down


def setup_inputs(seed: int = 0) -> dict:
    key = jax.random.key(seed)
    ks = iter(jax.random.split(key, 48))
    D = D_MODEL

    def nrm(shape, scale):
        return scale * jax.random.normal(next(ks), shape, jnp.float32)

    inp = {}
    inp['x'] = nrm((BATCH, SEQ, D), 1.0)
    inp['c'] = nrm((BATCH, D), 1.0)
    inp['ctx'] = nrm((BATCH, CTX_LEN, D), 1.0)
    inp['c_ctx'] = nrm((D,), 1.0)
    inp['mod_w'] = nrm((DEPTH, D, N_MOD * D), 0.5 * D ** -0.5)
    inp['mod_b'] = nrm((DEPTH, N_MOD * D), 0.05)
    inp['norm_mix_g'] = 1.0 + nrm((DEPTH, D), 0.05)
    inp['norm_ffn_g'] = 1.0 + nrm((DEPTH, D), 0.05)
    inp['ab_w_in'] = nrm((N_EVEN, D, AB_IN), D ** -0.5)
    inp['ab_w_out'] = nrm((N_EVEN, AB_OUT, D), AB_OUT ** -0.5)
    inp['lru_conv_w'] = nrm((N_EVEN, RNN_CONV, RNN_W), RNN_CONV ** -0.5)
    inp['lru_conv_b'] = nrm((N_EVEN, RNN_W), 0.02)
    inp['lru_wa'] = nrm((N_EVEN, 2, RNN_HEADS, RNN_HD, RNN_HD), RNN_HD ** -0.5)
    inp['lru_ba'] = nrm((N_EVEN, 2, RNN_W), 0.1)
    inp['lru_wx'] = nrm((N_EVEN, 2, RNN_HEADS, RNN_HD, RNN_HD), RNN_HD ** -0.5)
    inp['lru_bx'] = nrm((N_EVEN, 2, RNN_W), 0.1)
    a8 = jax.random.uniform(next(ks), (N_EVEN, 2, RNN_W), jnp.float32, 0.9, 0.999)
    s = a8 ** (1.0 / LRU_C)
    inp['lru_lambda'] = jnp.log(s) - jnp.log1p(-s)
    inp['pool_w'] = nrm((N_EVEN, POOL_GROUPS, POOL_GD, POOL_GD), POOL_GD ** -0.5)
    inp['pool_scale'] = 1.0 + nrm((N_EVEN, POOL_W), 0.1)
    inp['cd_w_in'] = nrm((N_ODD, D, CD_IN), D ** -0.5)
    inp['cd_w_out'] = nrm((N_ODD, CD_OUT, D), CD_OUT ** -0.5)
    inp['diff_lq1'] = nrm((N_ODD, ATT_HD), 0.1)
    inp['diff_lk1'] = nrm((N_ODD, ATT_HD), 0.1)
    inp['diff_lq2'] = nrm((N_ODD, ATT_HD), 0.1)
    inp['diff_lk2'] = nrm((N_ODD, ATT_HD), 0.1)
    inp['diff_subln_g'] = 1.0 + nrm((N_ODD, ATT_VD), 0.05)
    inp['conf_dw_w'] = nrm((N_ODD, CONF_K, CONF_W), CONF_K ** -0.5)
    inp['conf_dw_b'] = nrm((N_ODD, CONF_W), 0.02)
    inp['conf_ln_g'] = 1.0 + nrm((N_ODD, CONF_W), 0.05)
    inp['conf_ln_b'] = nrm((N_ODD, CONF_W), 0.02)
    inp['ffn_w_up'] = nrm((DEPTH, D, 2 * D_FF), D ** -0.5)
    inp['ffn_conv_w'] = nrm((DEPTH, FFN_K, 2 * D_FF), FFN_K ** -0.5)
    inp['ffn_conv_b'] = nrm((DEPTH, 2 * D_FF), 0.02)
    inp['ffn_w_down'] = nrm((DEPTH, D_FF, D), D_FF ** -0.5)
    inp['final_g'] = 1.0 + nrm((D,), 0.05)
    return inp


def reference(x, c, ctx, c_ctx, mod_w, mod_b, norm_mix_g, norm_ffn_g, ab_w_in, ab_w_out, lru_conv_w, lru_conv_b,
              lru_wa, lru_ba, lru_wx, lru_bx, lru_lambda, pool_w, pool_scale, cd_w_in, cd_w_out, diff_lq1, diff_lk1,
              diff_lq2, diff_lk2, diff_subln_g, conf_dw_w, conf_dw_b, conf_ln_g, conf_ln_b, ffn_w_up, ffn_conv_w,
              ffn_conv_b, ffn_w_down, final_g):
    L = x.shape[1]
    ROWS = L // GRID_W
    row = jnp.repeat(jnp.arange(ROWS), GRID_W).astype(jnp.float32)
    col = jnp.tile(jnp.arange(GRID_W), ROWS).astype(jnp.float32)
    inv = ROPE_BASE ** (-jnp.arange(ROPE_AX, dtype=jnp.float32) / ROPE_AX)
    ang_r = row[:, None] * inv
    ang_c = col[:, None] * inv
    rope = (jnp.cos(ang_r), jnp.sin(ang_r), jnp.cos(ang_c), jnp.sin(ang_c))

    for i in range(DEPTH):
        j = i // 2
        need_ctx = i < DEPTH - 1
        sh1, sc1, g1, sh2, sc2, g2 = [m[:, None, :] for m in adaln(c, mod_w[i], mod_b[i])]
        sh1c, sc1c, g1c, sh2c, sc2c, g2c = adaln(c_ctx, mod_w[i], mod_b[i])
        h = modulate(x, norm_mix_g[i], sh1, sc1)
        hc = modulate(ctx, norm_mix_g[i], sh1c, sc1c)
        if i % 2 == 0:
            y, yc = mix_ab(h, hc, ab_w_in[j], ab_w_out[j], lru_conv_w[j], lru_conv_b[j], lru_wa[j], lru_ba[j],
                           lru_wx[j], lru_bx[j], lru_lambda[j], pool_w[j], pool_scale[j], need_ctx)
        else:
            lambda_init = 0.8 - 0.6 * math.exp(-0.3 * i)
            y, yc = mix_cd(h, hc, cd_w_in[j], cd_w_out[j], diff_lq1[j], diff_lk1[j], diff_lq2[j], diff_lk2[j],
                           diff_subln_g[j], conf_dw_w[j], conf_dw_b[j], conf_ln_g[j], conf_ln_b[j], rope,
                           lambda_init, need_ctx)
        x = x + g1 * y
        x = x + g2 * conv_ffn(modulate(x, norm_ffn_g[i], sh2, sc2), ffn_w_up[i], ffn_conv_w[i], ffn_conv_b[i], ffn_w_down[i])
        if need_ctx:
            ctx = ctx + g1c * yc
            ctx = ctx + g2c * conv_ffn(modulate(ctx, norm_ffn_g[i], sh2c, sc2c), ffn_w_up[i], ffn_conv_w[i],
                                       ffn_conv_b[i], ffn_w_down[i])
    return rmsnorm(x, final_g)
```

```python
import functools
import math

import jax
import jax.numpy as jnp
from jax import lax
from jax.experimental import pallas as pl
from jax.experimental.pallas import tpu as pltpu

F32 = jnp.float32
BF16 = jnp.bfloat16
SDS = jax.ShapeDtypeStruct

EPS = 1e-6
N_MOD = 6
GRID_W = 64
ROPE_BASE = 10000.0
ROPE_AX = 16
ATT_HD = 64
ATT_VD = 128
LRU_C = 8.0
POOL_WINDOWS = (2, 4, 8, 16)
CONF_K = 31
FFN_CHUNK = 256
LANES = 128
HALO = 16
VMEM_LIMIT = 56 * 1024 * 1024


def _cparams(n_grid):
    return pltpu.CompilerParams(dimension_semantics=("arbitrary",) * n_grid, vmem_limit_bytes=VMEM_LIMIT)


def _tile(L):
    return min(512, L)


def _dot(a, b):
    return jnp.dot(a, b, preferred_element_type=F32)


def _modulate(xf, g, sh, sc):
    ms = jnp.mean(xf * xf, axis=-1, keepdims=True)
    return (xf * lax.rsqrt(ms + EPS) * g) * (1.0 + sc) + sh


def _shift_rows(v, o):
    n = v.shape[0]
    return pltpu.roll(v, (-o) % n, 0)


def _resident(shape):
    nd = len(shape)
    return pl.BlockSpec(shape, lambda *_: (0,) * nd, pipeline_mode=pl.Buffered(1))


def _halo_specs(T, L, width, rows=HALO):
    r = T // rows
    last = L // rows - 1
    prev = pl.BlockSpec((1, rows, width), lambda b, i: (b, jnp.maximum(i * r - 1, 0), 0))
    nxt = pl.BlockSpec((1, rows, width), lambda b, i: (b, jnp.minimum((i + 1) * r, last), 0))
    return prev, nxt


def _mod_body(c_ref, w_ref, b_ref, o_ref):
    s = c_ref[...]
    s = (s * jax.nn.sigmoid(s)).astype(BF16)
    o_ref[0] = _dot(s, w_ref[0].astype(BF16)) + b_ref[0]


def _adaln_all(cv8, mod_w, mod_b):
    depth, D, N = mod_w.shape
    tn = 1024
    return pl.pallas_call(
        _mod_body, grid=(depth, N // tn),
        in_specs=[pl.BlockSpec((8, D), lambda l, j: (0, 0)),
                  pl.BlockSpec((1, D, tn), lambda l, j: (l, 0, j)),
                  pl.BlockSpec((1, 1, tn), lambda l, j: (l, 0, j))],
        out_specs=pl.BlockSpec((1, 8, tn), lambda l, j: (l, 0, j)),
        out_shape=SDS((depth, 8, N), F32),
        compiler_params=_cparams(2), name="adaln",
    )(cv8, mod_w, mod_b.reshape(depth, 1, N))


def _proj_body(x_ref, g_ref, sh_ref, sc_ref, w_ref, *out_refs, splits):
    h = _modulate(x_ref[0], g_ref[...], sh_ref[0], sc_ref[0]).astype(BF16)
    for (s, wd), o in zip(splits, out_refs):
        o[0] = _dot(h, w_ref[:, s:s + wd]).astype(o.dtype)


def _proj(x, g, sh, sc, w, splits, name):
    B, L, D = x.shape
    T = _tile(L)
    vec = pl.BlockSpec((1, 1, D), lambda b, i: (b, 0, 0))
    return pl.pallas_call(
        functools.partial(_proj_body, splits=splits), grid=(B, L // T),
        in_specs=[pl.BlockSpec((1, T, D), lambda b, i: (b, i, 0)), _resident((1, D)), vec, vec, _resident(w.shape)],
        out_specs=[pl.BlockSpec((1, T, wd), lambda b, i: (b, i, 0)) for _, wd in splits],
        out_shape=[SDS((B, L, wd), BF16) for _, wd in splits],
        compiler_params=_cparams(2), name=name,
    )(x, g, sh, sc, w)


def _cd_in_body(x_ref, g_ref, sh_ref, sc_ref, w_ref, cos_ref, sin_ref, q_ref, k_ref, v_ref, u_ref, *, att_w):
    h = _modulate(x_ref[0], g_ref[...], sh_ref[0], sc_ref[0]).astype(BF16)
    cos_t = cos_ref[...]
    sin_t = sin_ref[...]
    low = (lax.broadcasted_iota(jnp.int32, cos_t.shape, 1) % (2 * ROPE_AX)) < ROPE_AX

    def rope(z):
        partner = jnp.where(low, pltpu.roll(z, LANES - ROPE_AX, 1), pltpu.roll(z, ROPE_AX, 1))
        return z * cos_t + partner * sin_t

    for c in range(att_w // LANES):
        sl = slice(c * LANES, (c + 1) * LANES)
        q_ref[0, :, sl] = (rope(_dot(h, w_ref[:, sl])) * (ATT_HD ** -0.5)).astype(BF16)
        k_ref[0, :, sl] = rope(_dot(h, w_ref[:, att_w + c * LANES:att_w + (c + 1) * LANES])).astype(BF16)
    v_ref[0] = _dot(h, w_ref[:, 2 * att_w:3 * att_w]).astype(BF16)
    u_ref[0] = _dot(h, w_ref[:, 3 * att_w:]).astype(BF16)


def _cd_in(x, g, sh, sc, w, cos_t, sin_t, att_w):
    B, L, D = x.shape
    T = _tile(L)
    conf2 = w.shape[1] - 3 * att_w
    vec = pl.BlockSpec((1, 1, D), lambda b, i: (b, 0, 0))
    tab = pl.BlockSpec((T, LANES), lambda b, i: (i, 0))
    widths = (att_w, att_w, att_w, conf2)
    return pl.pallas_call(
        functools.partial(_cd_in_body, att_w=att_w), grid=(B, L // T),
        in_specs=[pl.BlockSpec((1, T, D), lambda b, i: (b, i, 0)), _resident((1, D)), vec, vec, _resident(w.shape),
                  tab, tab],
        out_specs=[pl.BlockSpec((1, T, wd), lambda b, i: (b, i, 0)) for wd in widths],
        out_shape=[SDS((B, L, wd), BF16) for wd in widths],
        compiler_params=_cparams(2), name="cd_in",
    )(x, g, sh, sc, w, cos_t, sin_t)


def _lru_body(zf_ref, zfp_ref, zfn_ref, zb_ref, zbp_ref, zbn_ref, cw_ref, cb_ref, wg_ref, ba_ref, bx_ref, lam_ref,
              h0_ref, hf_ref, hb_ref, hfin_ref, carry_ref, a_s, b_s, hl_s, al_s, *, T, pad_lo):
    i = pl.program_id(1)
    nt = pl.num_programs(1)
    S = T // 8
    nch = a_s.shape[1]
    N = T + 2 * HALO

    @pl.when(i == 0)
    def _():
        carry_ref[...] = h0_ref[:, 0]

    tiles = ((zf_ref, zfp_ref, zfn_ref, i), (zb_ref, zbp_ref, zbn_ref, nt - 1 - i))
    for d, (z_ref, zp_ref, zn_ref, ti) in enumerate(tiles):
        keep_prev = jnp.where(ti == 0, 0.0, 1.0)
        keep_next = jnp.where(ti == nt - 1, 0.0, 1.0)
        zext = jnp.concatenate([zp_ref[0].astype(F32) * keep_prev, z_ref[0].astype(F32),
                                zn_ref[0].astype(F32) * keep_next], axis=0)
        xa = cb_ref[...]
        for k in range(cw_ref.shape[0]):
            xa = xa + cw_ref[k:k + 1, :] * _shift_rows(zext, HALO - pad_lo + k)[:T]
        for c in range(nch):
            sl = slice(c * LANES, (c + 1) * LANES)
            xc = xa[:, sl]
            gates = _dot(xc.astype(BF16), wg_ref[d, c])
            r = jax.nn.sigmoid(gates[:, :LANES] + ba_ref[d, :, sl])
            gi = jax.nn.sigmoid(gates[:, LANES:] + bx_ref[d, :, sl])
            log_a = (-LRU_C) * r * jax.nn.softplus(-lam_ref[d, :, sl])
            a = jnp.exp(log_a)
            a_s[d, c] = a
            b_s[d, c] = jnp.sqrt(-jnp.tanh(log_a) * (a * a + 1.0)) * (gi * xc)

    def p1(j, st):
        hs, cum = st
        nh, ncum = [], []
        for d in range(2):
            idx = pl.ds(j if d == 0 else S - 1 - j, 8, stride=S)
            for c in range(nch):
                a = a_s[d, c, idx, :]
                h = a * hs[d * nch + c] + b_s[d, c, idx, :]
                ac = a * cum[d * nch + c]
                hl_s[d, c, idx, :] = h
                al_s[d, c, idx, :] = ac
                nh.append(h)
                ncum.append(ac)
        return tuple(nh), tuple(ncum)

    zeros = tuple(jnp.zeros((8, LANES), F32) for _ in range(2 * nch))
    ones = tuple(jnp.ones((8, LANES), F32) for _ in range(2 * nch))
    h_end, a_end = lax.fori_loop(0, S, p1, (zeros, ones))

    out_refs = (hf_ref, hb_ref)
    for d in range(2):
        order = range(8) if d == 0 else range(7, -1, -1)
        for c in range(nch):
            sl = slice(c * LANES, (c + 1) * LANES)
            carry = carry_ref[d, :, sl]
            he, ae = h_end[d * nch + c], a_end[d * nch + c]
            for s in order:
                rows = slice(s * S, (s + 1) * S)
                out_refs[d][0, rows, sl] = (hl_s[d, c, rows, :] + al_s[d, c, rows, :] * carry).astype(BF16)
                carry = ae[s:s + 1, :] * carry + he[s:s + 1, :]
            carry_ref[d, :, sl] = carry
    hfin_ref[:, 0] = carry_ref[...]


def _lru(zx, conv_w, conv_b, wg, ba, bx, lam, h0):
    B, L, W = zx.shape
    T = _tile(L)
    nt = L // T
    nch = W // LANES
    K = conv_w.shape[0]
    fwd = pl.BlockSpec((1, T, W), lambda b, i: (b, i, 0))
    bwd = pl.BlockSpec((1, T, W), lambda b, i: (b, nt - 1 - i, 0))
    r = T // HALO
    last = L // HALO - 1
    fprev, fnext = _halo_specs(T, L, W)
    bprev = pl.BlockSpec((1, HALO, W), lambda b, i: (b, jnp.maximum((nt - 1 - i) * r - 1, 0), 0))
    bnext = pl.BlockSpec((1, HALO, W), lambda b, i: (b, jnp.minimum((nt - i) * r, last), 0))
    state = pl.BlockSpec((2, 1, 1, W), lambda b, i: (0, b, 0, 0))
    scr = pltpu.VMEM((2, nch, T, LANES), F32)
    return pl.pallas_call(
        functools.partial(_lru_body, T=T, pad_lo=K // 2), grid=(B, nt),
        in_specs=[fwd, fprev, fnext, bwd, bprev, bnext, _resident(conv_w.shape), _resident((1, W)),
                  _resident(wg.shape), _resident(ba.shape), _resident(bx.shape), _resident(lam.shape), state],
        out_specs=[fwd, bwd, state],
        out_shape=[SDS((B, L, W), BF16), SDS((B, L, W), BF16), SDS((2, B, 1, W), F32)],
        scratch_shapes=[pltpu.VMEM((2, 1, W), F32), scr, scr, scr, scr],
        compiler_params=_cparams(2), name="lru",
    )(zx, zx, zx, zx, zx, zx, conv_w, conv_b.reshape(1, W), wg, ba, bx, lam, h0)


def _ab_out_body(x_ref, hf_ref, hb_ref, zg_ref, zp_ref, zpp_ref, zpn_ref, wa_ref, wb_ref, pbd_ref, ps_ref, g1_ref,
                 o_ref, *, T, L):
    i = pl.program_id(1)
    nt = pl.num_programs(1)
    ya = (hf_ref[0].astype(F32) + hb_ref[0].astype(F32)) * jax.nn.gelu(zg_ref[0].astype(F32))
    y = _dot(ya.astype(BF16), wa_ref[...])

    keep_prev = jnp.where(i == 0, 0.0, 1.0)
    keep_next = jnp.where(i == nt - 1, 0.0, 1.0)
    xc = zp_ref[0].astype(F32)
    xe = jnp.concatenate([zpp_ref[0].astype(F32) * keep_prev, xc, zpn_ref[0].astype(F32) * keep_next], axis=0)
    sums = [xe + _shift_rows(xe, -1)]
    for half in (1, 2, 4):
        sums.append(_shift_rows(sums[-1], -half) + _shift_rows(sums[-1], half))
    lane = lax.broadcasted_iota(jnp.int32, xc.shape, 1)
    tpos = i * T + lax.broadcasted_iota(jnp.int32, xc.shape, 0)
    gd = xc.shape[1] // len(POOL_WINDOWS)
    mean = None
    for gi in range(len(POOL_WINDOWS) - 1, -1, -1):
        w = POOL_WINDOWS[gi]
        cnt = (jnp.minimum(tpos + (w - w // 2), L) - jnp.maximum(tpos - w // 2, 0)).astype(F32)
        m = sums[gi][HALO:HALO + T] / cnt
        mean = m if mean is None else jnp.where(lane < (gi + 1) * gd, m, mean)
    yb = _dot((mean - xc).astype(BF16), pbd_ref[...]) * ps_ref[...]
    y = y + _dot(yb.astype(BF16), wb_ref[...])
    o_ref[0] = x_ref[0] + g1_ref[0] * y


def _ab_out(x, hf, hb, zg, zp, w_a, w_b, pool_bd, pool_scale, g1):
    B, L, D = x.shape
    T = _tile(L)
    RW, PW = hf.shape[2], zp.shape[2]
    row = lambda wd: pl.BlockSpec((1, T, wd), lambda b, i: (b, i, 0))
    pprev, pnext = _halo_specs(T, L, PW)
    return pl.pallas_call(
        functools.partial(_ab_out_body, T=T, L=L), grid=(B, L // T),
        in_specs=[row(D), row(RW), row(RW), row(RW), row(PW), pprev, pnext, _resident(w_a.shape),
                  _resident(w_b.shape), _resident(pool_bd.shape), _resident((1, PW)),
                  pl.BlockSpec((1, 1, D), lambda b, i: (b, 0, 0))],
        out_specs=row(D), out_shape=SDS((B, L, D), F32),
        compiler_params=_cparams(2), name="ab_out",
    )(x, hf, hb, zg, zp, zp, zp, w_a, w_b, pool_bd, pool_scale.reshape(1, PW), g1)


def _ffn_body(x_ref, xp_ref, xn_ref, g_ref, sh_ref, sc_ref, g2_ref, wl_ref, wg_ref, cwl_ref, cwg_ref, cbl_ref,
              cbg_ref, wd_ref, fg_ref, o_ref, acc_ref, *, T, final):
    i = pl.program_id(1)
    nt = pl.num_programs(1)
    g, sh, sc = g_ref[...], sh_ref[0], sc_ref[0]
    hp = _modulate(xp_ref[0], g, sh, sc) * jnp.where(i == 0, 0.0, 1.0)
    hn = _modulate(xn_ref[0], g, sh, sc) * jnp.where(i == nt - 1, 0.0, 1.0)
    hext = jnp.concatenate([hp, _modulate(x_ref[0], g, sh, sc), hn], axis=0).astype(BF16)
    H = xp_ref.shape[1]

    def conv(u, cw, cb):
        return (cw[0:1] * _shift_rows(u, -1) + cw[1:2] * u + cw[2:3] * _shift_rows(u, 1))[H:H + T] + cb

    acc_ref[...] = jnp.zeros_like(acc_ref)

    def chunk(f, _):
        lin = conv(_dot(hext, wl_ref[f]), cwl_ref[f], cbl_ref[f])
        gate = conv(_dot(hext, wg_ref[f]), cwg_ref[f], cbg_ref[f])
        act = (gate * jax.nn.sigmoid(gate) * lin).astype(BF16)
        acc_ref[...] += _dot(act, wd_ref[f])
        return 0

    lax.fori_loop(0, wl_ref.shape[0], chunk, 0)
    out = x_ref[0] + g2_ref[0] * acc_ref[...]
    if final:
        out = out * lax.rsqrt(jnp.mean(out * out, axis=-1, keepdims=True) + EPS) * fg_ref[...]
    o_ref[0] = out


def _ffn(x, g, sh, sc, g2, wl, wg, cwl, cwg, cbl, cbg, wd, final_g, final):
    B, L, D = x.shape
    T = _tile(L)
    H = 8
    vec = pl.BlockSpec((1, 1, D), lambda b, i: (b, 0, 0))
    xprev, xnext = _halo_specs(T, L, D, rows=H)
    row = pl.BlockSpec((1, T, D), lambda b, i: (b, i, 0))
    return pl.pallas_call(
        functools.partial(_ffn_body, T=T, final=final), grid=(B, L // T),
        in_specs=[row, xprev, xnext, _resident((1, D)), vec, vec, vec, _resident(wl.shape), _resident(wg.shape),
                  _resident(cwl.shape), _resident(cwg.shape), _resident(cbl.shape), _resident(cbg.shape),
                  _resident(wd.shape), _resident((1, D))],
        out_specs=row, out_shape=SDS((B, L, D), F32),
        scratch_shapes=[pltpu.VMEM((T, D), F32)],
        compiler_params=_cparams(2), name="ffn",
    )(x, x, x, g, sh, sc, g2, wl, wg, cwl, cwg, cbl, cbg, wd, final_g)


def _attn_body(q_ref, k_ref, v_ref, kc_ref, vc_ref, lq1_ref, lk1_ref, lq2_ref, lk2_ref, sg_ref, o_ref, *,
               Tk, lambda_init):
    q = q_ref[0]
    Tq = q.shape[0]
    lane = lax.broadcasted_iota(jnp.int32, q.shape, 1)
    zero = jnp.zeros_like(q)
    qs = (jnp.where(lane < ATT_HD, q, zero), jnp.where(lane >= ATT_HD, q, zero))
    nt_dims = (((1,), (1,)), ((), ()))

    def update(kb, vb, st):
        new = []
        for comp in range(2):
            m, l, acc = st[comp]
            s = lax.dot_general(qs[comp], kb, nt_dims, preferred_element_type=F32)
            mn = jnp.maximum(m, jnp.max(s, axis=-1, keepdims=True))
            alpha = jnp.exp(m - mn)
            p = jnp.exp(s - mn)
            l = alpha * l + jnp.sum(p, axis=-1, keepdims=True)
            acc = alpha * acc + _dot(p.astype(BF16), vb)
            new.append((mn, l, acc))
        return tuple(new)

    def step(j, st):
        rows = pl.ds(pl.multiple_of(j * Tk, Tk), Tk)
        return update(k_ref[0, rows, :], v_ref[0, rows, :], st)

    init = tuple((jnp.full((Tq, 1), -jnp.inf, F32), jnp.zeros((Tq, 1), F32), jnp.zeros((Tq, ATT_VD), F32))
                 for _ in range(2))
    st = lax.fori_loop(0, k_ref.shape[1] // Tk, step, init)
    st = update(kc_ref[0], vc_ref[0], st)

    lam = (jnp.exp(jnp.sum(lq1_ref[...] * lk1_ref[...], axis=-1, keepdims=True))
           - jnp.exp(jnp.sum(lq2_ref[...] * lk2_ref[...], axis=-1, keepdims=True)) + lambda_init)
    (_, l1, a1), (_, l2, a2) = st
    o = a1 / l1 - lam * (a2 / l2)
    y = o * lax.rsqrt(jnp.mean(o * o, axis=-1, keepdims=True) + EPS) * sg_ref[...] * (1.0 - lambda_init)
    o_ref[0] = y.astype(o_ref.dtype)


def _attn(q, k, v, kc, vc, lq1, lk1, lq2, lk2, subln_g, lambda_init):
    B, L, W = q.shape
    Lc = kc.shape[1]
    H = W // ATT_VD
    Tq = _tile(L)
    Tk = _tile(L)
    qspec = pl.BlockSpec((1, Tq, ATT_VD), lambda b, h, i: (b, i, h))
    kv = pl.BlockSpec((1, L, ATT_VD), lambda b, h, i: (b, 0, h))
    kvc = pl.BlockSpec((1, Lc, ATT_VD), lambda b, h, i: (b, 0, h))
    small = _resident((1, ATT_HD))
    return pl.pallas_call(
        functools.partial(_attn_body, Tk=Tk, lambda_init=lambda_init), grid=(B, H, L // Tq),
        in_specs=[qspec, kv, kv, kvc, kvc, small, small, small, small, _resident((1, ATT_VD))],
        out_specs=qspec, out_shape=SDS((B, L, W), BF16),
        compiler_params=_cparams(3), name="diff_attn",
    )(q, k, v, kc, vc, lq1, lk1, lq2, lk2, subln_g)


def _cd_out_body(x_ref, yc_ref, u_ref, up_ref, un_ref, dww_ref, dwb_ref, lng_ref, lnb_ref, wc_ref, wd_ref, g1_ref,
                 o_ref, *, T):
    i = pl.program_id(1)
    nt = pl.num_programs(1)
    CW = dww_ref.shape[1]
    keep_prev = jnp.where(i == 0, 0.0, 1.0)
    keep_next = jnp.where(i == nt - 1, 0.0, 1.0)
    ue = jnp.concatenate([up_ref[0].astype(F32) * keep_prev, u_ref[0].astype(F32),
                          un_ref[0].astype(F32) * keep_next], axis=0)
    ge = ue[:, :CW] * jax.nn.sigmoid(ue[:, CW:])
    base = HALO - CONF_K // 2
    conv = dwb_ref[...]
    for r in range(8):
        shifted = _shift_rows(ge, r) if r else ge
        for k in range(CONF_K):
            off = base + k
            if off % 8 == r:
                conv = conv + dww_ref[k:k + 1, :] * shifted[off - r:off - r + T]
    mu = jnp.mean(conv, axis=-1, keepdims=True)
    cen = conv - mu
    var = jnp.mean(cen * cen, axis=-1, keepdims=True)
    ln = cen * lax.rsqrt(var + EPS) * lng_ref[...] + lnb_ref[...]
    yd = ln * jax.nn.sigmoid(ln)
    y = _dot(yc_ref[0], wc_ref[...]) + _dot(yd.astype(BF16), wd_ref[...])
    o_ref[0] = x_ref[0] + g1_ref[0] * y


def _cd_out(x, yc, u, dw_w, dw_b, ln_g, ln_b, w_c, w_d, g1):
    B, L, D = x.shape
    T = _tile(L)
    AW, UW, CW = yc.shape[2], u.shape[2], dw_w.shape[1]
    row = lambda wd: pl.BlockSpec((1, T, wd), lambda b, i: (b, i, 0))
    uprev, unext = _halo_specs(T, L, UW)
    return pl.pallas_call(
        functools.partial(_cd_out_body, T=T), grid=(B, L // T),
        in_specs=[row(D), row(AW), row(UW), uprev, unext, _resident(dw_w.shape), _resident((1, CW)),
                  _resident((1, CW)), _resident((1, CW)), _resident(w_c.shape), _resident(w_d.shape),
                  pl.BlockSpec((1, 1, D), lambda b, i: (b, 0, 0))],
        out_specs=row(D), out_shape=SDS((B, L, D), F32),
        compiler_params=_cparams(2), name="cd_out",
    )(x, yc, u, u, u, dw_w, dw_b.reshape(1, CW), ln_g.reshape(1, CW), ln_b.reshape(1, CW), w_c, w_d, g1)


def _block_diag(blocks):
    n, r, c = blocks.shape
    eye = jnp.eye(n, dtype=blocks.dtype)
    return (eye[:, None, :, None] * blocks[:, :, None, :]).reshape(n * r, n * c)


def _lru_gate_weights(wa, wx):
    ndir, heads, hd, _ = wa.shape
    per = LANES // hd

    def chunked(w):
        w = w.reshape(ndir * heads // per, per, hd, hd)
        return jax.vmap(_block_diag)(w).reshape(ndir, heads // per, LANES, LANES)

    return jnp.concatenate([chunked(wa), chunked(wx)], axis=-1).astype(BF16)


def _rope_tables(L):
    rows = L // GRID_W
    row = jnp.repeat(jnp.arange(rows), GRID_W).astype(F32)
    col = jnp.tile(jnp.arange(GRID_W), rows).astype(F32)
    inv = ROPE_BASE ** (-jnp.arange(ROPE_AX, dtype=F32) / ROPE_AX)
    ang_r = row[:, None] * inv
    ang_c = col[:, None] * inv
    ang = jnp.concatenate([ang_r, ang_r, ang_c, ang_c] * (LANES // (4 * ROPE_AX)), axis=-1)
    sign = jnp.tile(jnp.concatenate([-jnp.ones((ROPE_AX,), F32), jnp.ones((ROPE_AX,), F32)]), LANES // (2 * ROPE_AX))
    return jnp.cos(ang), jnp.sin(ang) * sign


def _ffn_weights(w_up, conv_w, conv_b, w_down):
    D, two_ff = w_up.shape
    ff = two_ff // 2
    nf = ff // FFN_CHUNK

    def cols(a):
        return a.reshape(a.shape[0], nf, FFN_CHUNK).transpose(1, 0, 2)

    wl, wg = cols(w_up[:, :ff]).astype(BF16), cols(w_up[:, ff:]).astype(BF16)
    cwl, cwg = cols(conv_w[:, :ff]), cols(conv_w[:, ff:])
    cbl, cbg = cols(conv_b[None, :ff]), cols(conv_b[None, ff:])
    wd = w_down.reshape(nf, FFN_CHUNK, D).astype(BF16)
    return wl, wg, cwl, cwg, cbl, cbg, wd


def kernel(x, c, ctx, c_ctx, mod_w, mod_b, norm_mix_g, norm_ffn_g, ab_w_in, ab_w_out, lru_conv_w, lru_conv_b, lru_wa, lru_ba, lru_wx, lru_bx, lru_lambda, pool_w, pool_scale, cd_w_in, cd_w_out, diff_lq1, diff_lk1, diff_lq2, diff_lk2, diff_subln_g, conf_dw_w, conf_dw_b, conf_ln_g, conf_ln_b, ffn_w_up, ffn_conv_w, ffn_conv_b, ffn_w_down, final_g):
    B, L, D = x.shape
    depth = mod_w.shape[0]
    rnn_w = lru_conv_w.shape[-1]
    pool_width = pool_scale.shape[-1]
    att_w = cd_w_out.shape[1] - conf_dw_w.shape[-1]

    cv8 = jnp.zeros((8, D), F32).at[:B].set(c).at[B].set(c_ctx)
    mods = _adaln_all(cv8, mod_w, mod_b)
    final_row = final_g.reshape(1, D)

    for i in range(depth):
        j = i // 2
        need_ctx = i < depth - 1
        m = mods[i].reshape(8, N_MOD, D)
        sh1, sc1, g1, sh2, sc2, g2 = [m[:B, n][:, None, :] for n in range(N_MOD)]
        sh1c, sc1c, g1c, sh2c, sc2c, g2c = [jnp.broadcast_to(m[B, n][None, None, :], (B, 1, D)) for n in range(N_MOD)]
        gm = norm_mix_g[i].reshape(1, D)
        gf = norm_ffn_g[i].reshape(1, D)
        ffn_w = _ffn_weights(ffn_w_up[i], ffn_conv_w[i], ffn_conv_b[i], ffn_w_down[i])

        if i % 2 == 0:
            w_in = ab_w_in[j].astype(BF16)
            w_out = ab_w_out[j].astype(BF16)
            w_a, w_b = w_out[:rnn_w], w_out[rnn_w:]
            splits = ((0, rnn_w), (rnn_w, rnn_w), (2 * rnn_w, pool_width))
            wg = _lru_gate_weights(lru_wa[j], lru_wx[j])
            ba = lru_ba[j].reshape(2, 1, rnn_w)
            bx = lru_bx[j].reshape(2, 1, rnn_w)
            lam = lru_lambda[j].reshape(2, 1, rnn_w)
            pool_bd = _block_diag(pool_w[j]).astype(BF16)

            zx_c, zg_c, zp_c = _proj(ctx, gm, sh1c, sc1c, w_in, splits, "ab_in_ctx")
            hf_c, hb_c, h_fin = _lru(zx_c, lru_conv_w[j], lru_conv_b[j], wg, ba, bx, lam,
                                     jnp.zeros((2, B, 1, rnn_w), F32))
            zx, zg, zp = _proj(x, gm, sh1, sc1, w_in, splits, "ab_in")
            hf, hb, _ = _lru(zx, lru_conv_w[j], lru_conv_b[j], wg, ba, bx, lam, h_fin)
            x = _ab_out(x, hf, hb, zg, zp, w_a, w_b, pool_bd, pool_scale[j], g1)
            if need_ctx:
                ctx = _ab_out(ctx, hf_c, hb_c, zg_c, zp_c, w_a, w_b, pool_bd, pool_scale[j], g1c)
        else:
            lambda_init = 0.8 - 0.6 * math.exp(-0.3 * i)
            w_in = cd_w_in[j].astype(BF16)
            w_out = cd_w_out[j].astype(BF16)
            w_c, w_d = w_out[:att_w], w_out[att_w:]
            cos_t, sin_t = _rope_tables(L)
            lq = [a[j].reshape(1, ATT_HD) for a in (diff_lq1, diff_lk1, diff_lq2, diff_lk2)]
            sg = diff_subln_g[j].reshape(1, ATT_VD)

            if need_ctx:
                raise NotImplementedError("an attention layer followed by further layers is not supported")
            q, k, v, u = _cd_in(x, gm, sh1, sc1, w_in, cos_t, sin_t, att_w)
            kc, vc = _proj(ctx, gm, sh1c, sc1c, w_in, ((att_w, att_w), (2 * att_w, att_w)), "cd_kv_ctx")
            yc = _attn(q, k, v, kc, vc, *lq, sg, lambda_init)
            x = _cd_out(x, yc, u, conf_dw_w[j], conf_dw_b[j], conf_ln_g[j], conf_ln_b[j], w_c, w_d, g1)

        x = _ffn(x, gf, sh2, sc2, g2, *ffn_w, final_row, final=(i == depth - 1))
        if need_ctx:
            ctx = _ffn(ctx, gf, sh2c, sc2c, g2c, *ffn_w, final_row, final=False)
    return x
```

```python
import functools
import math

import jax
import jax.numpy as jnp
from jax import lax
from jax.experimental import pallas as pl
from jax.experimental.pallas import tpu as pltpu

F32 = jnp.float32
BF16 = jnp.bfloat16
SDS = jax.ShapeDtypeStruct

EPS = 1e-6
N_MOD = 6
GRID_W = 64
ROPE_BASE = 10000.0
ROPE_AX = 16
ATT_HD = 64
ATT_VD = 128
LRU_C = 8.0
POOL_WINDOWS = (2, 4, 8, 16)
CONF_K = 31
FFN_CHUNK = 256
LANES = 128
MXU_N = 256
HALO = 16
VMEM_LIMIT = 56 * 1024 * 1024


def _cparams(n_grid):
    return pltpu.CompilerParams(dimension_semantics=("arbitrary",) * n_grid, vmem_limit_bytes=VMEM_LIMIT)


def _tile(L):
    return min(512, L)


def _dot(a, b):
    return jnp.dot(a, b, preferred_element_type=F32)


def _modulate(xf, g, sh, sc):
    ms = jnp.mean(xf * xf, axis=-1, keepdims=True)
    return (xf * lax.rsqrt(ms + EPS) * g) * (1.0 + sc) + sh


def _shift_rows(v, o):
    n = v.shape[0]
    return pltpu.roll(v, (-o) % n, 0)


def _resident(shape):
    nd = len(shape)
    return pl.BlockSpec(shape, lambda *_: (0,) * nd, pipeline_mode=pl.Buffered(1))


def _halo_specs(T, L, width, rows=HALO):
    r = T // rows
    last = L // rows - 1
    prev = pl.BlockSpec((1, rows, width), lambda b, i: (b, jnp.maximum(i * r - 1, 0), 0))
    nxt = pl.BlockSpec((1, rows, width), lambda b, i: (b, jnp.minimum((i + 1) * r, last), 0))
    return prev, nxt


def _mod_body(c_ref, w_ref, b_ref, o_ref):
    s = c_ref[...]
    s = (s * jax.nn.sigmoid(s)).astype(BF16)
    o_ref[0] = _dot(s, w_ref[0].astype(BF16)) + b_ref[0]


def _adaln_all(cv8, mod_w, mod_b):
    depth, D, N = mod_w.shape
    tn = 1024
    return pl.pallas_call(
        _mod_body, grid=(depth, N // tn),
        in_specs=[pl.BlockSpec((8, D), lambda l, j: (0, 0)),
                  pl.BlockSpec((1, D, tn), lambda l, j: (l, 0, j)),
                  pl.BlockSpec((1, 1, tn), lambda l, j: (l, 0, j))],
        out_specs=pl.BlockSpec((1, 8, tn), lambda l, j: (l, 0, j)),
        out_shape=SDS((depth, 8, N), F32),
        compiler_params=_cparams(2), name="adaln",
    )(cv8, mod_w, mod_b.reshape(depth, 1, N))


def _proj_body(x_ref, g_ref, sh_ref, sc_ref, w_ref, *out_refs, splits, transposed):
    h = _modulate(x_ref[0], g_ref[...], sh_ref[0], sc_ref[0]).astype(BF16)
    for (s, wd), tr, o in zip(splits, transposed, out_refs):
        z = _dot(h, w_ref[:, s:s + wd])
        o[0] = (z.T if tr else z).astype(o.dtype)


def _proj(x, g, sh, sc, w, splits, name, transposed=None):
    B, L, D = x.shape
    T = _tile(L)
    transposed = transposed or (False,) * len(splits)
    vec = pl.BlockSpec((1, 1, D), lambda b, i: (b, 0, 0))
    out_specs = [pl.BlockSpec((1, wd, T), lambda b, i: (b, 0, i)) if tr else pl.BlockSpec((1, T, wd), lambda b, i: (b, i, 0))
                 for (_, wd), tr in zip(splits, transposed)]
    out_shape = [SDS((B, wd, L) if tr else (B, L, wd), BF16) for (_, wd), tr in zip(splits, transposed)]
    return pl.pallas_call(
        functools.partial(_proj_body, splits=splits, transposed=transposed), grid=(B, L // T),
        in_specs=[pl.BlockSpec((1, T, D), lambda b, i: (b, i, 0)), _resident((1, D)), vec, vec, _resident(w.shape)],
        out_specs=out_specs, out_shape=out_shape,
        compiler_params=_cparams(2), name=name,
    )(x, g, sh, sc, w)


def _cd_in_body(x_ref, g_ref, sh_ref, sc_ref, w_ref, cos_ref, sin_ref, q_ref, k_ref, vt_ref, u_ref, *, att_w):
    h = _modulate(x_ref[0], g_ref[...], sh_ref[0], sc_ref[0]).astype(BF16)
    cos_t = cos_ref[...]
    sin_t = sin_ref[...]
    low = (lax.broadcasted_iota(jnp.int32, cos_t.shape, 1) % (2 * ROPE_AX)) < ROPE_AX

    def rope(z):
        partner = jnp.where(low, pltpu.roll(z, LANES - ROPE_AX, 1), pltpu.roll(z, ROPE_AX, 1))
        return z * cos_t + partner * sin_t

    q_scale = math.log2(math.e) * ATT_HD ** -0.5
    for c in range(att_w // LANES):
        sl = slice(c * LANES, (c + 1) * LANES)
        q_ref[0, :, sl] = (rope(_dot(h, w_ref[:, sl])) * q_scale).astype(BF16)
        k_ref[0, :, sl] = rope(_dot(h, w_ref[:, att_w + c * LANES:att_w + (c + 1) * LANES])).astype(BF16)
    vt_ref[0] = _dot(h, w_ref[:, 2 * att_w:3 * att_w]).T.astype(BF16)
    u_ref[0] = _dot(h, w_ref[:, 3 * att_w:]).astype(BF16)


def _cd_in(x, g, sh, sc, w, cos_t, sin_t, att_w):
    B, L, D = x.shape
    T = _tile(L)
    conf2 = w.shape[1] - 3 * att_w
    vec = pl.BlockSpec((1, 1, D), lambda b, i: (b, 0, 0))
    tab = pl.BlockSpec((T, LANES), lambda b, i: (i, 0))
    row = lambda wd: pl.BlockSpec((1, T, wd), lambda b, i: (b, i, 0))
    return pl.pallas_call(
        functools.partial(_cd_in_body, att_w=att_w), grid=(B, L // T),
        in_specs=[row(D), _resident((1, D)), vec, vec, _resident(w.shape), tab, tab],
        out_specs=[row(att_w), row(att_w), pl.BlockSpec((1, att_w, T), lambda b, i: (b, 0, i)), row(conf2)],
        out_shape=[SDS((B, L, att_w), BF16), SDS((B, L, att_w), BF16), SDS((B, att_w, L), BF16),
                   SDS((B, L, conf2), BF16)],
        compiler_params=_cparams(2), name="cd_in",
    )(x, g, sh, sc, w, cos_t, sin_t)


def _lru_body(zf_ref, zfp_ref, zfn_ref, zb_ref, zbp_ref, zbn_ref, cw_ref, cb_ref, wg_ref, ba_ref, bx_ref, lam_ref,
              h0_ref, hf_ref, hb_ref, hfin_ref, carry_ref, a_s, b_s, hl_s, al_s, *, T, pad_lo):
    i = pl.program_id(1)
    nt = pl.num_programs(1)
    S = T // 8
    nch = a_s.shape[1]
    N = T + 2 * HALO

    @pl.when(i == 0)
    def _():
        carry_ref[...] = h0_ref[:, 0]

    tiles = ((zf_ref, zfp_ref, zfn_ref, i), (zb_ref, zbp_ref, zbn_ref, nt - 1 - i))
    for d, (z_ref, zp_ref, zn_ref, ti) in enumerate(tiles):
        keep_prev = jnp.where(ti == 0, 0.0, 1.0)
        keep_next = jnp.where(ti == nt - 1, 0.0, 1.0)
        zext = jnp.concatenate([zp_ref[0].astype(F32) * keep_prev, z_ref[0].astype(F32),
                                zn_ref[0].astype(F32) * keep_next], axis=0)
        xa = cb_ref[...]
        for k in range(cw_ref.shape[0]):
            xa = xa + cw_ref[k:k + 1, :] * _shift_rows(zext, HALO - pad_lo + k)[:T]
        for c in range(nch):
            sl = slice(c * LANES, (c + 1) * LANES)
            xc = xa[:, sl]
            gates = _dot(xc.astype(BF16), wg_ref[d, c])
            r = jax.nn.sigmoid(gates[:, :LANES] + ba_ref[d, :, sl])
            gi = jax.nn.sigmoid(gates[:, LANES:] + bx_ref[d, :, sl])
            log_a = (-LRU_C) * r * jax.nn.softplus(-lam_ref[d, :, sl])
            a = jnp.exp(log_a)
            a_s[d, c] = a
            b_s[d, c] = jnp.sqrt(-jnp.tanh(log_a) * (a * a + 1.0)) * (gi * xc)

    def p1(j, st):
        hs, cum = st
        nh, ncum = [], []
        for d in range(2):
            idx = pl.ds(j if d == 0 else S - 1 - j, 8, stride=S)
            for c in range(nch):
                a = a_s[d, c, idx, :]
                h = a * hs[d * nch + c] + b_s[d, c, idx, :]
                ac = a * cum[d * nch + c]
                hl_s[d, c, idx, :] = h
                al_s[d, c, idx, :] = ac
                nh.append(h)
                ncum.append(ac)
        return tuple(nh), tuple(ncum)

    zeros = tuple(jnp.zeros((8, LANES), F32) for _ in range(2 * nch))
    ones = tuple(jnp.ones((8, LANES), F32) for _ in range(2 * nch))
    h_end, a_end = lax.fori_loop(0, S, p1, (zeros, ones))

    out_refs = (hf_ref, hb_ref)
    for d in range(2):
        order = range(8) if d == 0 else range(7, -1, -1)
        for c in range(nch):
            sl = slice(c * LANES, (c + 1) * LANES)
            carry = carry_ref[d, :, sl]
            he, ae = h_end[d * nch + c], a_end[d * nch + c]
            for s in order:
                rows = slice(s * S, (s + 1) * S)
                out_refs[d][0, rows, sl] = (hl_s[d, c, rows, :] + al_s[d, c, rows, :] * carry).astype(BF16)
                carry = ae[s:s + 1, :] * carry + he[s:s + 1, :]
            carry_ref[d, :, sl] = carry
    hfin_ref[:, 0] = carry_ref[...]


def _lru(zx, conv_w, conv_b, wg, ba, bx, lam, h0):
    B, L, W = zx.shape
    T = _tile(L)
    nt = L // T
    nch = W // LANES
    K = conv_w.shape[0]
    fwd = pl.BlockSpec((1, T, W), lambda b, i: (b, i, 0))
    bwd = pl.BlockSpec((1, T, W), lambda b, i: (b, nt - 1 - i, 0))
    r = T // HALO
    last = L // HALO - 1
    fprev, fnext = _halo_specs(T, L, W)
    bprev = pl.BlockSpec((1, HALO, W), lambda b, i: (b, jnp.maximum((nt - 1 - i) * r - 1, 0), 0))
    bnext = pl.BlockSpec((1, HALO, W), lambda b, i: (b, jnp.minimum((nt - i) * r, last), 0))
    state = pl.BlockSpec((2, 1, 1, W), lambda b, i: (0, b, 0, 0))
    scr = pltpu.VMEM((2, nch, T, LANES), F32)
    return pl.pallas_call(
        functools.partial(_lru_body, T=T, pad_lo=K // 2), grid=(B, nt),
        in_specs=[fwd, fprev, fnext, bwd, bprev, bnext, _resident(conv_w.shape), _resident((1, W)),
                  _resident(wg.shape), _resident(ba.shape), _resident(bx.shape), _resident(lam.shape), state],
        out_specs=[fwd, bwd, state],
        out_shape=[SDS((B, L, W), BF16), SDS((B, L, W), BF16), SDS((2, B, 1, W), F32)],
        scratch_shapes=[pltpu.VMEM((2, 1, W), F32), scr, scr, scr, scr],
        compiler_params=_cparams(2), name="lru",
    )(zx, zx, zx, zx, zx, zx, conv_w, conv_b.reshape(1, W), wg, ba, bx, lam, h0)


def _ab_out_body(x_ref, hf_ref, hb_ref, zg_ref, zp_ref, zpp_ref, zpn_ref, wa_ref, wb_ref, pbd_ref, ps_ref, g1_ref,
                 o_ref, *, T, L):
    i = pl.program_id(1)
    nt = pl.num_programs(1)
    ya = (hf_ref[0].astype(F32) + hb_ref[0].astype(F32)) * jax.nn.gelu(zg_ref[0].astype(F32))
    y = _dot(ya.astype(BF16), wa_ref[...])

    keep_prev = jnp.where(i == 0, 0.0, 1.0)
    keep_next = jnp.where(i == nt - 1, 0.0, 1.0)
    xc = zp_ref[0].astype(F32)
    xe = jnp.concatenate([zpp_ref[0].astype(F32) * keep_prev, xc, zpn_ref[0].astype(F32) * keep_next], axis=0)
    sums = [xe + _shift_rows(xe, -1)]
    for half in (1, 2, 4):
        sums.append(_shift_rows(sums[-1], -half) + _shift_rows(sums[-1], half))
    lane = lax.broadcasted_iota(jnp.int32, xc.shape, 1)
    tpos = i * T + lax.broadcasted_iota(jnp.int32, xc.shape, 0)
    gd = xc.shape[1] // len(POOL_WINDOWS)
    mean = None
    for gi in range(len(POOL_WINDOWS) - 1, -1, -1):
        w = POOL_WINDOWS[gi]
        cnt = (jnp.minimum(tpos + (w - w // 2), L) - jnp.maximum(tpos - w // 2, 0)).astype(F32)
        m = sums[gi][HALO:HALO + T] / cnt
        mean = m if mean is None else jnp.where(lane < (gi + 1) * gd, m, mean)
    yb = _dot((mean - xc).astype(BF16), pbd_ref[...]) * ps_ref[...]
    y = y + _dot(yb.astype(BF16), wb_ref[...])
    o_ref[0] = x_ref[0] + g1_ref[0] * y


def _ab_out(x, hf, hb, zg, zp, w_a, w_b, pool_bd, pool_scale, g1):
    B, L, D = x.shape
    T = _tile(L)
    RW, PW = hf.shape[2], zp.shape[2]
    row = lambda wd: pl.BlockSpec((1, T, wd), lambda b, i: (b, i, 0))
    pprev, pnext = _halo_specs(T, L, PW)
    return pl.pallas_call(
        functools.partial(_ab_out_body, T=T, L=L), grid=(B, L // T),
        in_specs=[row(D), row(RW), row(RW), row(RW), row(PW), pprev, pnext, _resident(w_a.shape),
                  _resident(w_b.shape), _resident(pool_bd.shape), _resident((1, PW)),
                  pl.BlockSpec((1, 1, D), lambda b, i: (b, 0, 0))],
        out_specs=row(D), out_shape=SDS((B, L, D), F32),
        compiler_params=_cparams(2), name="ab_out",
    )(x, hf, hb, zg, zp, zp, zp, w_a, w_b, pool_bd, pool_scale.reshape(1, PW), g1)


def _ffn_body(x_ref, xp_ref, xn_ref, g_ref, sh_ref, sc_ref, g2_ref, wl_ref, wg_ref, cwl_ref, cwg_ref, cbl_ref,
              cbg_ref, wd_ref, fg_ref, o_ref, acc_ref, hext_s, ua_s, ub_s, *, T, final):
    i = pl.program_id(1)
    nt = pl.num_programs(1)
    nf = wl_ref.shape[0]
    g, sh, sc = g_ref[...], sh_ref[0], sc_ref[0]
    hp = _modulate(xp_ref[0], g, sh, sc) * jnp.where(i == 0, 0.0, 1.0)
    hn = _modulate(xn_ref[0], g, sh, sc) * jnp.where(i == nt - 1, 0.0, 1.0)
    hext_s[...] = jnp.concatenate([hp, _modulate(x_ref[0], g, sh, sc), hn], axis=0).astype(BF16)
    H = xp_ref.shape[1]

    def conv(u, cw, cb):
        return (cw[0:1] * _shift_rows(u, -1) + cw[1:2] * u + cw[2:3] * _shift_rows(u, 1))[H:H + T] + cb

    def up(f, u_s):
        u_s[0] = _dot(hext_s[...], wl_ref[f])
        u_s[1] = _dot(hext_s[...], wg_ref[f])

    def down(f, u_s):
        lin = conv(u_s[0], cwl_ref[f], cbl_ref[f])
        gate = conv(u_s[1], cwg_ref[f], cbg_ref[f])
        act = (gate * jax.nn.sigmoid(gate) * lin).astype(BF16)
        acc_ref[...] += _dot(act, wd_ref[f])

    acc_ref[...] = jnp.zeros_like(acc_ref)
    up(0, ua_s)

    def pair(jj, _):
        up(2 * jj + 1, ub_s)
        down(2 * jj, ua_s)
        up(2 * jj + 2, ua_s)
        down(2 * jj + 1, ub_s)
        return 0

    lax.fori_loop(0, (nf - 1) // 2, pair, 0)
    if nf % 2 == 0:
        up(nf - 1, ub_s)
        down(nf - 2, ua_s)
        down(nf - 1, ub_s)
    else:
        down(nf - 1, ua_s)
    out = x_ref[0] + g2_ref[0] * acc_ref[...]
    if final:
        out = out * lax.rsqrt(jnp.mean(out * out, axis=-1, keepdims=True) + EPS) * fg_ref[...]
    o_ref[0] = out


def _ffn(x, g, sh, sc, g2, wl, wg, cwl, cwg, cbl, cbg, wd, final_g, final):
    B, L, D = x.shape
    T = _tile(L)
    H = 8
    vec = pl.BlockSpec((1, 1, D), lambda b, i: (b, 0, 0))
    xprev, xnext = _halo_specs(T, L, D, rows=H)
    row = pl.BlockSpec((1, T, D), lambda b, i: (b, i, 0))
    return pl.pallas_call(
        functools.partial(_ffn_body, T=T, final=final), grid=(B, L // T),
        in_specs=[row, xprev, xnext, _resident((1, D)), vec, vec, vec, _resident(wl.shape), _resident(wg.shape),
                  _resident(cwl.shape), _resident(cwg.shape), _resident(cbl.shape), _resident(cbg.shape),
                  _resident(wd.shape), _resident((1, D))],
        out_specs=row, out_shape=SDS((B, L, D), F32),
        scratch_shapes=[pltpu.VMEM((T, D), F32), pltpu.VMEM((T + 2 * H, D), BF16),
                        pltpu.VMEM((2, T + 2 * H, wl.shape[2]), F32), pltpu.VMEM((2, T + 2 * H, wl.shape[2]), F32)],
        compiler_params=_cparams(2), name="ffn",
    )(x, x, x, g, sh, sc, g2, wl, wg, cwl, cwg, cbl, cbg, wd, final_g)


def _attn_body(q_ref, k_ref, vt_ref, kc_ref, vct_ref, lq1_ref, lk1_ref, lq2_ref, lk2_ref, sg_ref, o_ref,
               qm_s, sa_s, sb_s, sc_s, m_s, l_s, acc_s, *, Tk, lambda_init):
    Tq = q_ref.shape[1]
    nk = k_ref.shape[1] // Tk
    nt_dims = (((1,), (1,)), ((), ()))
    cols = [slice(c * MXU_N, (c + 1) * MXU_N) for c in range(Tq // MXU_N)]

    q = q_ref[0]
    lane = lax.broadcasted_iota(jnp.int32, q.shape, 1)
    zero = jnp.zeros_like(q)
    qm_s[0] = jnp.where(lane < ATT_HD, q, zero)
    qm_s[1] = jnp.where(lane >= ATT_HD, q, zero)
    m_s[...] = jnp.full(m_s.shape, -jnp.inf, F32)
    l_s[...] = jnp.zeros(l_s.shape, F32)
    acc_s[...] = jnp.zeros(acc_s.shape, F32)

    def scores(kb, s_ref):
        for comp in range(2):
            for cs in cols:
                s_ref[comp, :, cs] = lax.dot_general(kb, qm_s[comp, cs, :], nt_dims, preferred_element_type=F32)

    def consume(s_ref, vtb):
        for comp in range(2):
            for cs in cols:
                s = s_ref[comp, :, cs]
                m_old = m_s[comp, :, cs]
                m_new = jnp.maximum(m_old, jnp.max(s, axis=0, keepdims=True))
                alpha = jnp.exp2(m_old - m_new)
                p = jnp.exp2(s - m_new)
                l_s[comp, :, cs] = alpha * l_s[comp, :, cs] + jnp.sum(p, axis=0, keepdims=True)
                acc_s[comp, :, cs] = alpha * acc_s[comp, :, cs] + _dot(vtb, p.astype(BF16))
                m_s[comp, :, cs] = m_new

    def kblock(j):
        return k_ref[0, pl.ds(pl.multiple_of(j * Tk, Tk), Tk), :]

    def vblock(j):
        return vt_ref[0, :, pl.ds(pl.multiple_of(j * Tk, Tk), Tk)]

    scores(kc_ref[0], sc_s)
    scores(kblock(0), sa_s)
    consume(sc_s, vct_ref[0])

    def pair(jj, _):
        scores(kblock(2 * jj + 1), sb_s)
        consume(sa_s, vblock(2 * jj))
        scores(kblock(2 * jj + 2), sa_s)
        consume(sb_s, vblock(2 * jj + 1))
        return 0

    lax.fori_loop(0, nk // 2 - 1, pair, 0)
    scores(kblock(nk - 1), sb_s)
    consume(sa_s, vblock(nk - 2))
    consume(sb_s, vblock(nk - 1))

    lam = (jnp.exp(jnp.sum(lq1_ref[...] * lk1_ref[...], axis=-1, keepdims=True))
           - jnp.exp(jnp.sum(lq2_ref[...] * lk2_ref[...], axis=-1, keepdims=True)) + lambda_init)
    o = acc_s[0] * (1.0 / l_s[0]) - acc_s[1] * (lam / l_s[1])
    y = o * lax.rsqrt(jnp.mean(o * o, axis=0, keepdims=True) + EPS) * (sg_ref[...] * (1.0 - lambda_init))
    o_ref[0] = y.T.astype(o_ref.dtype)


def _attn(q, k, vt, kc, vct, lq1, lk1, lq2, lk2, subln_g, lambda_init, Tq=512, Tk=512):
    B, L, W = q.shape
    Lc = kc.shape[1]
    H = W // ATT_VD
    Tq, Tk = min(Tq, L), min(Tk, L)
    qspec = pl.BlockSpec((1, Tq, ATT_VD), lambda b, h, i: (b, i, h))
    kspec = pl.BlockSpec((1, L, ATT_VD), lambda b, h, i: (b, 0, h))
    vspec = pl.BlockSpec((1, ATT_VD, L), lambda b, h, i: (b, h, 0))
    kcspec = pl.BlockSpec((1, Lc, ATT_VD), lambda b, h, i: (b, 0, h))
    vcspec = pl.BlockSpec((1, ATT_VD, Lc), lambda b, h, i: (b, h, 0))
    small = _resident((1, ATT_HD))
    assert (L // Tk) % 2 == 0 and Tq % MXU_N == 0
    score_slot = pltpu.VMEM((2, Tk, Tq), F32)
    return pl.pallas_call(
        functools.partial(_attn_body, Tk=Tk, lambda_init=lambda_init), grid=(B, H, L // Tq),
        in_specs=[qspec, kspec, vspec, kcspec, vcspec, small, small, small, small, _resident((ATT_VD, 1))],
        out_specs=qspec, out_shape=SDS((B, L, W), BF16),
        scratch_shapes=[pltpu.VMEM((2, Tq, ATT_VD), BF16), score_slot, score_slot, pltpu.VMEM((2, Lc, Tq), F32),
                        pltpu.VMEM((2, 1, Tq), F32), pltpu.VMEM((2, 1, Tq), F32), pltpu.VMEM((2, ATT_VD, Tq), F32)],
        compiler_params=_cparams(3), name="diff_attn",
    )(q, k, vt, kc, vct, lq1, lk1, lq2, lk2, subln_g)


def _cd_out_body(x_ref, yc_ref, u_ref, up_ref, un_ref, dww_ref, dwb_ref, lng_ref, lnb_ref, wc_ref, wd_ref, g1_ref,
                 o_ref, *, T):
    i = pl.program_id(1)
    nt = pl.num_programs(1)
    CW = dww_ref.shape[1]
    keep_prev = jnp.where(i == 0, 0.0, 1.0)
    keep_next = jnp.where(i == nt - 1, 0.0, 1.0)
    ue = jnp.concatenate([up_ref[0].astype(F32) * keep_prev, u_ref[0].astype(F32),
                          un_ref[0].astype(F32) * keep_next], axis=0)
    ge = ue[:, :CW] * jax.nn.sigmoid(ue[:, CW:])
    base = HALO - CONF_K // 2
    conv = dwb_ref[...]
    for r in range(8):
        shifted = _shift_rows(ge, r) if r else ge
        for k in range(CONF_K):
            off = base + k
            if off % 8 == r:
                conv = conv + dww_ref[k:k + 1, :] * shifted[off - r:off - r + T]
    mu = jnp.mean(conv, axis=-1, keepdims=True)
    cen = conv - mu
    var = jnp.mean(cen * cen, axis=-1, keepdims=True)
    ln = cen * lax.rsqrt(var + EPS) * lng_ref[...] + lnb_ref[...]
    yd = ln * jax.nn.sigmoid(ln)
    y = _dot(yc_ref[0], wc_ref[...]) + _dot(yd.astype(BF16), wd_ref[...])
    o_ref[0] = x_ref[0] + g1_ref[0] * y


def _cd_out(x, yc, u, dw_w, dw_b, ln_g, ln_b, w_c, w_d, g1):
    B, L, D = x.shape
    T = _tile(L)
    AW, UW, CW = yc.shape[2], u.shape[2], dw_w.shape[1]
    row = lambda wd: pl.BlockSpec((1, T, wd), lambda b, i: (b, i, 0))
    uprev, unext = _halo_specs(T, L, UW)
    return pl.pallas_call(
        functools.partial(_cd_out_body, T=T), grid=(B, L // T),
        in_specs=[row(D), row(AW), row(UW), uprev, unext, _resident(dw_w.shape), _resident((1, CW)),
                  _resident((1, CW)), _resident((1, CW)), _resident(w_c.shape), _resident(w_d.shape),
                  pl.BlockSpec((1, 1, D), lambda b, i: (b, 0, 0))],
        out_specs=row(D), out_shape=SDS((B, L, D), F32),
        compiler_params=_cparams(2), name="cd_out",
    )(x, yc, u, u, u, dw_w, dw_b.reshape(1, CW), ln_g.reshape(1, CW), ln_b.reshape(1, CW), w_c, w_d, g1)


def _block_diag(blocks):
    n, r, c = blocks.shape
    eye = jnp.eye(n, dtype=blocks.dtype)
    return (eye[:, None, :, None] * blocks[:, :, None, :]).reshape(n * r, n * c)


def _lru_gate_weights(wa, wx):
    ndir, heads, hd, _ = wa.shape
    per = LANES // hd

    def chunked(w):
        w = w.reshape(ndir * heads // per, per, hd, hd)
        return jax.vmap(_block_diag)(w).reshape(ndir, heads // per, LANES, LANES)

    return jnp.concatenate([chunked(wa), chunked(wx)], axis=-1).astype(BF16)


def _rope_tables(L):
    rows = L // GRID_W
    row = jnp.repeat(jnp.arange(rows), GRID_W).astype(F32)
    col = jnp.tile(jnp.arange(GRID_W), rows).astype(F32)
    inv = ROPE_BASE ** (-jnp.arange(ROPE_AX, dtype=F32) / ROPE_AX)
    ang_r = row[:, None] * inv
    ang_c = col[:, None] * inv
    ang = jnp.concatenate([ang_r, ang_r, ang_c, ang_c] * (LANES // (4 * ROPE_AX)), axis=-1)
    sign = jnp.tile(jnp.concatenate([-jnp.ones((ROPE_AX,), F32), jnp.ones((ROPE_AX,), F32)]), LANES // (2 * ROPE_AX))
    return jnp.cos(ang), jnp.sin(ang) * sign


def _ffn_weights(w_up, conv_w, conv_b, w_down):
    D, two_ff = w_up.shape
    ff = two_ff // 2
    nf = ff // FFN_CHUNK

    def cols(a):
        return a.reshape(a.shape[0], nf, FFN_CHUNK).transpose(1, 0, 2)

    wl, wg = cols(w_up[:, :ff]).astype(BF16), cols(w_up[:, ff:]).astype(BF16)
    cwl, cwg = cols(conv_w[:, :ff]), cols(conv_w[:, ff:])
    cbl, cbg = cols(conv_b[None, :ff]), cols(conv_b[None, ff:])
    wd = w_down.reshape(nf, FFN_CHUNK, D).astype(BF16)
    return wl, wg, cwl, cwg, cbl, cbg, wd


def kernel(x, c, ctx, c_ctx, mod_w, mod_b, norm_mix_g, norm_ffn_g, ab_w_in, ab_w_out, lru_conv_w, lru_conv_b, lru_wa, lru_ba, lru_wx, lru_bx, lru_lambda, pool_w, pool_scale, cd_w_in, cd_w_out, diff_lq1, diff_lk1, diff_lq2, diff_lk2, diff_subln_g, conf_dw_w, conf_dw_b, conf_ln_g, conf_ln_b, ffn_w_up, ffn_conv_w, ffn_conv_b, ffn_w_down, final_g):
    B, L, D = x.shape
    depth = mod_w.shape[0]
    rnn_w = lru_conv_w.shape[-1]
    pool_width = pool_scale.shape[-1]
    att_w = cd_w_out.shape[1] - conf_dw_w.shape[-1]

    cv8 = jnp.zeros((8, D), F32).at[:B].set(c).at[B].set(c_ctx)
    mods = _adaln_all(cv8, mod_w, mod_b)
    final_row = final_g.reshape(1, D)

    for i in range(depth):
        j = i // 2
        need_ctx = i < depth - 1
        m = mods[i].reshape(8, N_MOD, D)
        sh1, sc1, g1, sh2, sc2, g2 = [m[:B, n][:, None, :] for n in range(N_MOD)]
        sh1c, sc1c, g1c, sh2c, sc2c, g2c = [jnp.broadcast_to(m[B, n][None, None, :], (B, 1, D)) for n in range(N_MOD)]
        gm = norm_mix_g[i].reshape(1, D)
        gf = norm_ffn_g[i].reshape(1, D)
        ffn_w = _ffn_weights(ffn_w_up[i], ffn_conv_w[i], ffn_conv_b[i], ffn_w_down[i])

        if i % 2 == 0:
            w_in = ab_w_in[j].astype(BF16)
            w_out = ab_w_out[j].astype(BF16)
            w_a, w_b = w_out[:rnn_w], w_out[rnn_w:]
            splits = ((0, rnn_w), (rnn_w, rnn_w), (2 * rnn_w, pool_width))
            wg = _lru_gate_weights(lru_wa[j], lru_wx[j])
            ba = lru_ba[j].reshape(2, 1, rnn_w)
            bx = lru_bx[j].reshape(2, 1, rnn_w)
            lam = lru_lambda[j].reshape(2, 1, rnn_w)
            pool_bd = _block_diag(pool_w[j]).astype(BF16)

            zx_c, zg_c, zp_c = _proj(ctx, gm, sh1c, sc1c, w_in, splits, "ab_in_ctx")
            hf_c, hb_c, h_fin = _lru(zx_c, lru_conv_w[j], lru_conv_b[j], wg, ba, bx, lam,
                                     jnp.zeros((2, B, 1, rnn_w), F32))
            zx, zg, zp = _proj(x, gm, sh1, sc1, w_in, splits, "ab_in")
            hf, hb, _ = _lru(zx, lru_conv_w[j], lru_conv_b[j], wg, ba, bx, lam, h_fin)
            x = _ab_out(x, hf, hb, zg, zp, w_a, w_b, pool_bd, pool_scale[j], g1)
            if need_ctx:
                ctx = _ab_out(ctx, hf_c, hb_c, zg_c, zp_c, w_a, w_b, pool_bd, pool_scale[j], g1c)
        else:
            lambda_init = 0.8 - 0.6 * math.exp(-0.3 * i)
            w_in = cd_w_in[j].astype(BF16)
            w_out = cd_w_out[j].astype(BF16)
            w_c, w_d = w_out[:att_w], w_out[att_w:]
            cos_t, sin_t = _rope_tables(L)
            lq = [a[j].reshape(1, ATT_HD) for a in (diff_lq1, diff_lk1, diff_lq2, diff_lk2)]
            sg = diff_subln_g[j].reshape(ATT_VD, 1)

            if need_ctx:
                raise NotImplementedError("an attention layer followed by further layers is not supported")
            q, k, vt, u = _cd_in(x, gm, sh1, sc1, w_in, cos_t, sin_t, att_w)
            kc, vct = _proj(ctx, gm, sh1c, sc1c, w_in, ((att_w, att_w), (2 * att_w, att_w)), "cd_kv_ctx",
                            transposed=(False, True))
            yc = _attn(q, k, vt, kc, vct, *lq, sg, lambda_init)
            x = _cd_out(x, yc, u, conf_dw_w[j], conf_dw_b[j], conf_ln_g[j], conf_ln_b[j], w_c, w_d, g1)

        x = _ffn(x, gf, sh2, sc2, g2, *ffn_w, final_row, final=(i == depth - 1))
        if need_ctx:
            ctx = _ffn(ctx, gf, sh2c, sc2c, g2c, *ffn_w, final_row, final=False)
    return x
```

```python
import functools
import math

import jax
import jax.numpy as jnp
from jax import lax
from jax.experimental import pallas as pl
from jax.experimental.pallas import tpu as pltpu

F32 = jnp.float32
BF16 = jnp.bfloat16
SDS = jax.ShapeDtypeStruct

EPS = 1e-6
N_MOD = 6
GRID_W = 64
ROPE_BASE = 10000.0
ROPE_AX = 16
ATT_HD = 64
ATT_VD = 128
LRU_C = 8.0
POOL_WINDOWS = (2, 4, 8, 16)
CONF_K = 31
FFN_CHUNK = 256
FFN_TILE = 512
ATT_TQ = 1024
ATT_TK = 512
LANES = 128
MXU_N = 256
HALO = 16
VMEM_LIMIT = 56 * 1024 * 1024


def _cparams(n_grid):
    return pltpu.CompilerParams(dimension_semantics=("arbitrary",) * n_grid, vmem_limit_bytes=VMEM_LIMIT)


def _tile(L):
    return min(512, L)


def _dot(a, b):
    return jnp.dot(a, b, preferred_element_type=F32)


def _modulate(xf, g, sh, sc):
    ms = jnp.mean(xf * xf, axis=-1, keepdims=True)
    return (xf * lax.rsqrt(ms + EPS) * g) * (1.0 + sc) + sh


def _shift_rows(v, o):
    n = v.shape[0]
    return pltpu.roll(v, (-o) % n, 0)


def _resident(shape):
    nd = len(shape)
    return pl.BlockSpec(shape, lambda *_: (0,) * nd, pipeline_mode=pl.Buffered(1))


def _halo_specs(T, L, width, rows=HALO):
    r = T // rows
    last = L // rows - 1
    prev = pl.BlockSpec((1, rows, width), lambda b, i: (b, jnp.maximum(i * r - 1, 0), 0))
    nxt = pl.BlockSpec((1, rows, width), lambda b, i: (b, jnp.minimum((i + 1) * r, last), 0))
    return prev, nxt


def _mod_body(c_ref, w_ref, b_ref, o_ref):
    s = c_ref[...]
    s = (s * jax.nn.sigmoid(s)).astype(BF16)
    o_ref[0] = _dot(s, w_ref[0].astype(BF16)) + b_ref[0]


def _adaln_all(cv8, mod_w, mod_b):
    depth, D, N = mod_w.shape
    tn = 1024
    return pl.pallas_call(
        _mod_body, grid=(depth, N // tn),
        in_specs=[pl.BlockSpec((8, D), lambda l, j: (0, 0)),
                  pl.BlockSpec((1, D, tn), lambda l, j: (l, 0, j)),
                  pl.BlockSpec((1, 1, tn), lambda l, j: (l, 0, j))],
        out_specs=pl.BlockSpec((1, 8, tn), lambda l, j: (l, 0, j)),
        out_shape=SDS((depth, 8, N), F32),
        compiler_params=_cparams(2), name="adaln",
    )(cv8, mod_w, mod_b.reshape(depth, 1, N))


def _proj_body(x_ref, g_ref, sh_ref, sc_ref, w_ref, *out_refs, splits, transposed):
    h = _modulate(x_ref[0], g_ref[...], sh_ref[0], sc_ref[0]).astype(BF16)
    for (s, wd), tr, o in zip(splits, transposed, out_refs):
        z = _dot(h, w_ref[:, s:s + wd])
        o[0] = (z.T if tr else z).astype(o.dtype)


def _proj(x, g, sh, sc, w, splits, name, transposed=None):
    B, L, D = x.shape
    T = _tile(L)
    transposed = transposed or (False,) * len(splits)
    vec = pl.BlockSpec((1, 1, D), lambda b, i: (b, 0, 0))
    out_specs = [pl.BlockSpec((1, wd, T), lambda b, i: (b, 0, i)) if tr else pl.BlockSpec((1, T, wd), lambda b, i: (b, i, 0))
                 for (_, wd), tr in zip(splits, transposed)]
    out_shape = [SDS((B, wd, L) if tr else (B, L, wd), BF16) for (_, wd), tr in zip(splits, transposed)]
    return pl.pallas_call(
        functools.partial(_proj_body, splits=splits, transposed=transposed), grid=(B, L // T),
        in_specs=[pl.BlockSpec((1, T, D), lambda b, i: (b, i, 0)), _resident((1, D)), vec, vec, _resident(w.shape)],
        out_specs=out_specs, out_shape=out_shape,
        compiler_params=_cparams(2), name=name,
    )(x, g, sh, sc, w)


def _cd_in_body(x_ref, g_ref, sh_ref, sc_ref, w_ref, cos_ref, sin_ref, q_ref, k_ref, vt_ref, u_ref, *, att_w):
    h = _modulate(x_ref[0], g_ref[...], sh_ref[0], sc_ref[0]).astype(BF16)
    cos_t = cos_ref[...]
    sin_t = sin_ref[...]
    low = (lax.broadcasted_iota(jnp.int32, cos_t.shape, 1) % (2 * ROPE_AX)) < ROPE_AX

    def rope(z):
        partner = jnp.where(low, pltpu.roll(z, LANES - ROPE_AX, 1), pltpu.roll(z, ROPE_AX, 1))
        return z * cos_t + partner * sin_t

    q_scale = math.log2(math.e) * ATT_HD ** -0.5
    for c in range(att_w // LANES):
        sl = slice(c * LANES, (c + 1) * LANES)
        q_ref[0, :, sl] = (rope(_dot(h, w_ref[:, sl])) * q_scale).astype(BF16)
        k_ref[0, :, sl] = rope(_dot(h, w_ref[:, att_w + c * LANES:att_w + (c + 1) * LANES])).astype(BF16)
    vt_ref[0] = _dot(h, w_ref[:, 2 * att_w:3 * att_w]).T.astype(BF16)
    u_ref[0] = _dot(h, w_ref[:, 3 * att_w:]).astype(BF16)


def _cd_in(x, g, sh, sc, w, cos_t, sin_t, att_w):
    B, L, D = x.shape
    T = _tile(L)
    conf2 = w.shape[1] - 3 * att_w
    vec = pl.BlockSpec((1, 1, D), lambda b, i: (b, 0, 0))
    tab = pl.BlockSpec((T, LANES), lambda b, i: (i, 0))
    row = lambda wd: pl.BlockSpec((1, T, wd), lambda b, i: (b, i, 0))
    return pl.pallas_call(
        functools.partial(_cd_in_body, att_w=att_w), grid=(B, L // T),
        in_specs=[row(D), _resident((1, D)), vec, vec, _resident(w.shape), tab, tab],
        out_specs=[row(att_w), row(att_w), pl.BlockSpec((1, att_w, T), lambda b, i: (b, 0, i)), row(conf2)],
        out_shape=[SDS((B, L, att_w), BF16), SDS((B, L, att_w), BF16), SDS((B, att_w, L), BF16),
                   SDS((B, L, conf2), BF16)],
        compiler_params=_cparams(2), name="cd_in",
    )(x, g, sh, sc, w, cos_t, sin_t)


def _lru_body(zf_ref, zfp_ref, zfn_ref, zb_ref, zbp_ref, zbn_ref, cw_ref, cb_ref, wg_ref, ba_ref, bx_ref, lam_ref,
              h0_ref, hf_ref, hb_ref, hfin_ref, carry_ref, z_s, xa_s, a_s, b_s, hl_s, al_s, *, T, pad_lo):
    i = pl.program_id(1)
    nt = pl.num_programs(1)
    S = T // 8
    nch = a_s.shape[1]
    K = cw_ref.shape[0]
    base = HALO - pad_lo

    @pl.when(i == 0)
    def _():
        carry_ref[...] = h0_ref[:, 0]

    tiles = ((zf_ref, zfp_ref, zfn_ref, i), (zb_ref, zbp_ref, zbn_ref, nt - 1 - i))
    for d, (z_ref, zp_ref, zn_ref, ti) in enumerate(tiles):
        keep_prev = jnp.where(ti == 0, 0.0, 1.0)
        keep_next = jnp.where(ti == nt - 1, 0.0, 1.0)
        for c in range(nch):
            sl = slice(c * LANES, (c + 1) * LANES)
            z_s[d, c, 0:HALO] = zp_ref[0, :, sl].astype(F32) * keep_prev
            z_s[d, c, HALO:HALO + T] = z_ref[0, :, sl].astype(F32)
            z_s[d, c, HALO + T:] = zn_ref[0, :, sl].astype(F32) * keep_next

            wk = [jnp.broadcast_to(cw_ref[k:k + 1, sl], (8, LANES)) for k in range(K)]
            bias = jnp.broadcast_to(cb_ref[:, sl], (8, LANES))
            win = [z_s[d, c, pl.ds(base + k, 8, stride=S), :] for k in range(K - 1)]
            for j in range(S):
                win.append(z_s[d, c, pl.ds(base + j + K - 1, 8, stride=S), :])
                xa = bias
                for k in range(K):
                    xa = xa + wk[k] * win[k]
                xa_s[d, c, j * 8:(j + 1) * 8, :] = xa
                win.pop(0)

            xc = xa_s[d, c]
            gates = _dot(xc.astype(BF16), wg_ref[d, c])
            r = 0.5 * jnp.tanh(0.5 * (gates[:, :LANES] + ba_ref[d, :, sl])) + 0.5
            gi = 0.5 * jnp.tanh(0.5 * (gates[:, LANES:] + bx_ref[d, :, sl])) + 0.5
            neg_log_a = r * (LRU_C * jax.nn.softplus(-lam_ref[d, :, sl]))
            a = jnp.exp(-neg_log_a)
            one_m_a2 = jnp.tanh(neg_log_a) * (a * a + 1.0)
            root = jnp.where(one_m_a2 > 0.0, one_m_a2 * lax.rsqrt(one_m_a2), 0.0)
            a_s[d, c] = a
            b_s[d, c] = root * (gi * xc)

    def rows_of(d, j):
        return pl.ds(pl.multiple_of((j if d == 0 else S - 1 - j) * 8, 8), 8)

    def p1(j, st):
        hs, cum = st
        nh, ncum = [], []
        for d in range(2):
            rows = rows_of(d, j)
            for c in range(nch):
                a = a_s[d, c, rows, :]
                h = a * hs[d * nch + c] + b_s[d, c, rows, :]
                ac = a * cum[d * nch + c]
                hl_s[d, c, rows, :] = h
                al_s[d, c, rows, :] = ac
                nh.append(h)
                ncum.append(ac)
        return tuple(nh), tuple(ncum)

    zeros = tuple(jnp.zeros((8, LANES), F32) for _ in range(2 * nch))
    ones = tuple(jnp.ones((8, LANES), F32) for _ in range(2 * nch))
    h_end, a_end = lax.fori_loop(0, S, p1, (zeros, ones))

    sub = lax.broadcasted_iota(jnp.int32, (8, LANES), 0)
    cmats = []
    for d in range(2):
        order = range(8) if d == 0 else range(7, -1, -1)
        for c in range(nch):
            sl = slice(c * LANES, (c + 1) * LANES)
            carry = carry_ref[d, :, sl]
            he, ae = h_end[d * nch + c], a_end[d * nch + c]
            cmat = jnp.zeros((8, LANES), F32)
            for s in order:
                cmat = jnp.where(sub == s, carry, cmat)
                carry = ae[s:s + 1, :] * carry + he[s:s + 1, :]
            carry_ref[d, :, sl] = carry
            cmats.append(cmat)
    hfin_ref[:, 0] = carry_ref[...]

    def p2(j, _):
        rows = pl.ds(pl.multiple_of(j * 8, 8), 8)
        for d in range(2):
            for c in range(nch):
                hl_s[d, c, rows, :] = hl_s[d, c, rows, :] + al_s[d, c, rows, :] * cmats[d * nch + c]
        return 0

    lax.fori_loop(0, S, p2, 0)

    out_refs = (hf_ref, hb_ref)
    for d in range(2):
        for c in range(nch):
            for s in range(8):
                out_refs[d][0, s * S:(s + 1) * S, c * LANES:(c + 1) * LANES] = (
                    hl_s[d, c, pl.ds(s, S, stride=8), :].astype(BF16))


def _lru(zx, conv_w, conv_b, wg, ba, bx, lam, h0):
    B, L, W = zx.shape
    T = _tile(L)
    nt = L // T
    nch = W // LANES
    K = conv_w.shape[0]
    fwd = pl.BlockSpec((1, T, W), lambda b, i: (b, i, 0))
    bwd = pl.BlockSpec((1, T, W), lambda b, i: (b, nt - 1 - i, 0))
    r = T // HALO
    last = L // HALO - 1
    fprev, fnext = _halo_specs(T, L, W)
    bprev = pl.BlockSpec((1, HALO, W), lambda b, i: (b, jnp.maximum((nt - 1 - i) * r - 1, 0), 0))
    bnext = pl.BlockSpec((1, HALO, W), lambda b, i: (b, jnp.minimum((nt - i) * r, last), 0))
    state = pl.BlockSpec((2, 1, 1, W), lambda b, i: (0, b, 0, 0))
    scr = pltpu.VMEM((2, nch, T, LANES), F32)
    return pl.pallas_call(
        functools.partial(_lru_body, T=T, pad_lo=K // 2), grid=(B, nt),
        in_specs=[fwd, fprev, fnext, bwd, bprev, bnext, _resident(conv_w.shape), _resident((1, W)),
                  _resident(wg.shape), _resident(ba.shape), _resident(bx.shape), _resident(lam.shape), state],
        out_specs=[fwd, bwd, state],
        out_shape=[SDS((B, L, W), BF16), SDS((B, L, W), BF16), SDS((2, B, 1, W), F32)],
        scratch_shapes=[pltpu.VMEM((2, 1, W), F32), pltpu.VMEM((2, nch, T + 2 * HALO, LANES), F32),
                        scr, scr, scr, scr, scr],
        compiler_params=_cparams(2), name="lru",
    )(zx, zx, zx, zx, zx, zx, conv_w, conv_b.reshape(1, W), wg, ba, bx, lam, h0)


def _ab_out_body(x_ref, hf_ref, hb_ref, zg_ref, zp_ref, zpp_ref, zpn_ref, wa_ref, wb_ref, pbd_ref, ps_ref, g1_ref,
                 o_ref, *, T, L):
    i = pl.program_id(1)
    nt = pl.num_programs(1)
    ya = (hf_ref[0].astype(F32) + hb_ref[0].astype(F32)) * jax.nn.gelu(zg_ref[0].astype(F32))
    y = _dot(ya.astype(BF16), wa_ref[...])

    keep_prev = jnp.where(i == 0, 0.0, 1.0)
    keep_next = jnp.where(i == nt - 1, 0.0, 1.0)
    xc = zp_ref[0].astype(F32)
    xe = jnp.concatenate([zpp_ref[0].astype(F32) * keep_prev, xc, zpn_ref[0].astype(F32) * keep_next], axis=0)
    sums = [xe + _shift_rows(xe, -1)]
    for half in (1, 2, 4):
        sums.append(_shift_rows(sums[-1], -half) + _shift_rows(sums[-1], half))
    lane = lax.broadcasted_iota(jnp.int32, xc.shape, 1)
    tpos = i * T + lax.broadcasted_iota(jnp.int32, xc.shape, 0)
    gd = xc.shape[1] // len(POOL_WINDOWS)
    mean = None
    for gi in range(len(POOL_WINDOWS) - 1, -1, -1):
        w = POOL_WINDOWS[gi]
        cnt = (jnp.minimum(tpos + (w - w // 2), L) - jnp.maximum(tpos - w // 2, 0)).astype(F32)
        m = sums[gi][HALO:HALO + T] / cnt
        mean = m if mean is None else jnp.where(lane < (gi + 1) * gd, m, mean)
    yb = _dot((mean - xc).astype(BF16), pbd_ref[...]) * ps_ref[...]
    y = y + _dot(yb.astype(BF16), wb_ref[...])
    o_ref[0] = x_ref[0] + g1_ref[0] * y


def _ab_out(x, hf, hb, zg, zp, w_a, w_b, pool_bd, pool_scale, g1):
    B, L, D = x.shape
    T = _tile(L)
    RW, PW = hf.shape[2], zp.shape[2]
    row = lambda wd: pl.BlockSpec((1, T, wd), lambda b, i: (b, i, 0))
    pprev, pnext = _halo_specs(T, L, PW)
    return pl.pallas_call(
        functools.partial(_ab_out_body, T=T, L=L), grid=(B, L // T),
        in_specs=[row(D), row(RW), row(RW), row(RW), row(PW), pprev, pnext, _resident(w_a.shape),
                  _resident(w_b.shape), _resident(pool_bd.shape), _resident((1, PW)),
                  pl.BlockSpec((1, 1, D), lambda b, i: (b, 0, 0))],
        out_specs=row(D), out_shape=SDS((B, L, D), F32),
        compiler_params=_cparams(2), name="ab_out",
    )(x, hf, hb, zg, zp, zp, zp, w_a, w_b, pool_bd, pool_scale.reshape(1, PW), g1)


def _ffn_body(x_ref, xp_ref, xn_ref, g_ref, sh_ref, sc_ref, g2_ref, wl_ref, wg_ref, cwl_ref, cwg_ref, cbl_ref,
              cbg_ref, wd_ref, fg_ref, o_ref, acc_ref, hext_s, ua_s, ub_s, *, T, final):
    i = pl.program_id(1)
    nt = pl.num_programs(1)
    nf = wl_ref.shape[0]
    g, sh, sc = g_ref[...], sh_ref[0], sc_ref[0]
    hp = _modulate(xp_ref[0], g, sh, sc) * jnp.where(i == 0, 0.0, 1.0)
    hn = _modulate(xn_ref[0], g, sh, sc) * jnp.where(i == nt - 1, 0.0, 1.0)
    hext_s[...] = jnp.concatenate([hp, _modulate(x_ref[0], g, sh, sc), hn], axis=0).astype(BF16)
    H = xp_ref.shape[1]

    def conv(u, cw, cb):
        return (cw[0:1] * _shift_rows(u, -1) + cw[1:2] * u + cw[2:3] * _shift_rows(u, 1))[H:H + T] + cb

    def up(f, u_s):
        u_s[0] = _dot(hext_s[...], wl_ref[f])
        u_s[1] = _dot(hext_s[...], wg_ref[f])

    def down(f, u_s):
        lin = conv(u_s[0], cwl_ref[f], cbl_ref[f])
        gate = conv(u_s[1], cwg_ref[f], cbg_ref[f])
        act = (gate * jax.nn.sigmoid(gate) * lin).astype(BF16)
        acc_ref[...] += _dot(act, wd_ref[f])

    acc_ref[...] = jnp.zeros_like(acc_ref)
    up(0, ua_s)

    def pair(jj, _):
        up(2 * jj + 1, ub_s)
        down(2 * jj, ua_s)
        up(2 * jj + 2, ua_s)
        down(2 * jj + 1, ub_s)
        return 0

    lax.fori_loop(0, (nf - 1) // 2, pair, 0)
    if nf % 2 == 0:
        up(nf - 1, ub_s)
        down(nf - 2, ua_s)
        down(nf - 1, ub_s)
    else:
        down(nf - 1, ua_s)
    out = x_ref[0] + g2_ref[0] * acc_ref[...]
    if final:
        out = out * lax.rsqrt(jnp.mean(out * out, axis=-1, keepdims=True) + EPS) * fg_ref[...]
    o_ref[0] = out


def _ffn(x, g, sh, sc, g2, wl, wg, cwl, cwg, cbl, cbg, wd, final_g, final, tile=FFN_TILE):
    B, L, D = x.shape
    T = min(tile, L)
    H = 8
    vec = pl.BlockSpec((1, 1, D), lambda b, i: (b, 0, 0))
    xprev, xnext = _halo_specs(T, L, D, rows=H)
    row = pl.BlockSpec((1, T, D), lambda b, i: (b, i, 0))
    return pl.pallas_call(
        functools.partial(_ffn_body, T=T, final=final), grid=(B, L // T),
        in_specs=[row, xprev, xnext, _resident((1, D)), vec, vec, vec, _resident(wl.shape), _resident(wg.shape),
                  _resident(cwl.shape), _resident(cwg.shape), _resident(cbl.shape), _resident(cbg.shape),
                  _resident(wd.shape), _resident((1, D))],
        out_specs=row, out_shape=SDS((B, L, D), F32),
        scratch_shapes=[pltpu.VMEM((T, D), F32), pltpu.VMEM((T + 2 * H, D), BF16),
                        pltpu.VMEM((2, T + 2 * H, wl.shape[2]), F32), pltpu.VMEM((2, T + 2 * H, wl.shape[2]), F32)],
        compiler_params=_cparams(2), name="ffn",
    )(x, x, x, g, sh, sc, g2, wl, wg, cwl, cwg, cbl, cbg, wd, final_g)


def _attn_body(q_ref, k_ref, vt_ref, kc_ref, vct_ref, lq1_ref, lk1_ref, lq2_ref, lk2_ref, sg_ref, o_ref,
               qm_s, sa_s, sb_s, sc_s, m_s, l_s, acc_s, *, Tk, lambda_init):
    Tq = q_ref.shape[1]
    nk = k_ref.shape[1] // Tk
    nt_dims = (((1,), (1,)), ((), ()))
    cols = [slice(c * MXU_N, (c + 1) * MXU_N) for c in range(Tq // MXU_N)]

    q = q_ref[0]
    lane = lax.broadcasted_iota(jnp.int32, q.shape, 1)
    zero = jnp.zeros_like(q)
    qm_s[0] = jnp.where(lane < ATT_HD, q, zero)
    qm_s[1] = jnp.where(lane >= ATT_HD, q, zero)
    m_s[...] = jnp.full(m_s.shape, -jnp.inf, F32)
    l_s[...] = jnp.zeros(l_s.shape, F32)
    acc_s[...] = jnp.zeros(acc_s.shape, F32)

    def scores(kb, s_ref):
        for comp in range(2):
            for cs in cols:
                s_ref[comp, :, cs] = lax.dot_general(kb, qm_s[comp, cs, :], nt_dims, preferred_element_type=F32)

    def consume(s_ref, vtb):
        for comp in range(2):
            for cs in cols:
                s = s_ref[comp, :, cs]
                m_old = m_s[comp, :, cs]
                m_new = jnp.maximum(m_old, jnp.max(s, axis=0, keepdims=True))
                alpha = jnp.exp2(m_old - m_new)
                p = jnp.exp2(s - m_new)
                l_s[comp, :, cs] = alpha * l_s[comp, :, cs] + jnp.sum(p, axis=0, keepdims=True)
                acc_s[comp, :, cs] = alpha * acc_s[comp, :, cs] + _dot(vtb, p.astype(BF16))
                m_s[comp, :, cs] = m_new

    def kblock(j):
        return k_ref[0, pl.ds(pl.multiple_of(j * Tk, Tk), Tk), :]

    def vblock(j):
        return vt_ref[0, :, pl.ds(pl.multiple_of(j * Tk, Tk), Tk)]

    scores(kc_ref[0], sc_s)
    scores(kblock(0), sa_s)
    consume(sc_s, vct_ref[0])

    def pair(jj, _):
        scores(kblock(2 * jj + 1), sb_s)
        consume(sa_s, vblock(2 * jj))
        scores(kblock(2 * jj + 2), sa_s)
        consume(sb_s, vblock(2 * jj + 1))
        return 0

    lax.fori_loop(0, nk // 2 - 1, pair, 0)
    scores(kblock(nk - 1), sb_s)
    consume(sa_s, vblock(nk - 2))
    consume(sb_s, vblock(nk - 1))

    lam = (jnp.exp(jnp.sum(lq1_ref[...] * lk1_ref[...], axis=-1, keepdims=True))
           - jnp.exp(jnp.sum(lq2_ref[...] * lk2_ref[...], axis=-1, keepdims=True)) + lambda_init)
    o = acc_s[0] * (1.0 / l_s[0]) - acc_s[1] * (lam / l_s[1])
    y = o * lax.rsqrt(jnp.mean(o * o, axis=0, keepdims=True) + EPS) * (sg_ref[...] * (1.0 - lambda_init))
    o_ref[0] = y.T.astype(o_ref.dtype)


def _attn(q, k, vt, kc, vct, lq1, lk1, lq2, lk2, subln_g, lambda_init, Tq=ATT_TQ, Tk=ATT_TK):
    B, L, W = q.shape
    Lc = kc.shape[1]
    H = W // ATT_VD
    Tq, Tk = min(Tq, L), min(Tk, L)
    qspec = pl.BlockSpec((1, Tq, ATT_VD), lambda b, h, i: (b, i, h))
    kspec = pl.BlockSpec((1, L, ATT_VD), lambda b, h, i: (b, 0, h))
    vspec = pl.BlockSpec((1, ATT_VD, L), lambda b, h, i: (b, h, 0))
    kcspec = pl.BlockSpec((1, Lc, ATT_VD), lambda b, h, i: (b, 0, h))
    vcspec = pl.BlockSpec((1, ATT_VD, Lc), lambda b, h, i: (b, h, 0))
    small = _resident((1, ATT_HD))
    assert (L // Tk) % 2 == 0 and Tq % MXU_N == 0
    score_slot = pltpu.VMEM((2, Tk, Tq), F32)
    return pl.pallas_call(
        functools.partial(_attn_body, Tk=Tk, lambda_init=lambda_init), grid=(B, H, L // Tq),
        in_specs=[qspec, kspec, vspec, kcspec, vcspec, small, small, small, small, _resident((ATT_VD, 1))],
        out_specs=qspec, out_shape=SDS((B, L, W), BF16),
        scratch_shapes=[pltpu.VMEM((2, Tq, ATT_VD), BF16), score_slot, score_slot, pltpu.VMEM((2, Lc, Tq), F32),
                        pltpu.VMEM((2, 1, Tq), F32), pltpu.VMEM((2, 1, Tq), F32), pltpu.VMEM((2, ATT_VD, Tq), F32)],
        compiler_params=_cparams(3), name="diff_attn",
    )(q, k, vt, kc, vct, lq1, lk1, lq2, lk2, subln_g)


def _cd_out_body(x_ref, yc_ref, u_ref, up_ref, un_ref, dww_ref, dwb_ref, lng_ref, lnb_ref, wc_ref, wd_ref, g1_ref,
                 o_ref, *, T):
    i = pl.program_id(1)
    nt = pl.num_programs(1)
    CW = dww_ref.shape[1]
    keep_prev = jnp.where(i == 0, 0.0, 1.0)
    keep_next = jnp.where(i == nt - 1, 0.0, 1.0)
    ue = jnp.concatenate([up_ref[0].astype(F32) * keep_prev, u_ref[0].astype(F32),
                          un_ref[0].astype(F32) * keep_next], axis=0)
    ge = ue[:, :CW] * jax.nn.sigmoid(ue[:, CW:])
    base = HALO - CONF_K // 2
    conv = dwb_ref[...]
    for r in range(8):
        shifted = _shift_rows(ge, r) if r else ge
        for k in range(CONF_K):
            off = base + k
            if off % 8 == r:
                conv = conv + dww_ref[k:k + 1, :] * shifted[off - r:off - r + T]
    mu = jnp.mean(conv, axis=-1, keepdims=True)
    cen = conv - mu
    var = jnp.mean(cen * cen, axis=-1, keepdims=True)
    ln = cen * lax.rsqrt(var + EPS) * lng_ref[...] + lnb_ref[...]
    yd = ln * jax.nn.sigmoid(ln)
    y = _dot(yc_ref[0], wc_ref[...]) + _dot(yd.astype(BF16), wd_ref[...])
    o_ref[0] = x_ref[0] + g1_ref[0] * y


def _cd_out(x, yc, u, dw_w, dw_b, ln_g, ln_b, w_c, w_d, g1):
    B, L, D = x.shape
    T = _tile(L)
    AW, UW, CW = yc.shape[2], u.shape[2], dw_w.shape[1]
    row = lambda wd: pl.BlockSpec((1, T, wd), lambda b, i: (b, i, 0))
    uprev, unext = _halo_specs(T, L, UW)
    return pl.pallas_call(
        functools.partial(_cd_out_body, T=T), grid=(B, L // T),
        in_specs=[row(D), row(AW), row(UW), uprev, unext, _resident(dw_w.shape), _resident((1, CW)),
                  _resident((1, CW)), _resident((1, CW)), _resident(w_c.shape), _resident(w_d.shape),
                  pl.BlockSpec((1, 1, D), lambda b, i: (b, 0, 0))],
        out_specs=row(D), out_shape=SDS((B, L, D), F32),
        compiler_params=_cparams(2), name="cd_out",
    )(x, yc, u, u, u, dw_w, dw_b.reshape(1, CW), ln_g.reshape(1, CW), ln_b.reshape(1, CW), w_c, w_d, g1)


def _block_diag(blocks):
    n, r, c = blocks.shape
    eye = jnp.eye(n, dtype=blocks.dtype)
    return (eye[:, None, :, None] * blocks[:, :, None, :]).reshape(n * r, n * c)


def _lru_gate_weights(wa, wx):
    ndir, heads, hd, _ = wa.shape
    per = LANES // hd

    def chunked(w):
        w = w.reshape(ndir * heads // per, per, hd, hd)
        return jax.vmap(_block_diag)(w).reshape(ndir, heads // per, LANES, LANES)

    return jnp.concatenate([chunked(wa), chunked(wx)], axis=-1).astype(BF16)


def _rope_tables(L):
    rows = L // GRID_W
    row = jnp.repeat(jnp.arange(rows), GRID_W).astype(F32)
    col = jnp.tile(jnp.arange(GRID_W), rows).astype(F32)
    inv = ROPE_BASE ** (-jnp.arange(ROPE_AX, dtype=F32) / ROPE_AX)
    ang_r = row[:, None] * inv
    ang_c = col[:, None] * inv
    ang = jnp.concatenate([ang_r, ang_r, ang_c, ang_c] * (LANES // (4 * ROPE_AX)), axis=-1)
    sign = jnp.tile(jnp.concatenate([-jnp.ones((ROPE_AX,), F32), jnp.ones((ROPE_AX,), F32)]), LANES // (2 * ROPE_AX))
    return jnp.cos(ang), jnp.sin(ang) * sign


def _ffn_weights(w_up, conv_w, conv_b, w_down):
    D, two_ff = w_up.shape
    ff = two_ff // 2
    nf = ff // FFN_CHUNK

    def cols(a):
        return a.reshape(a.shape[0], nf, FFN_CHUNK).transpose(1, 0, 2)

    wl, wg = cols(w_up[:, :ff]).astype(BF16), cols(w_up[:, ff:]).astype(BF16)
    cwl, cwg = cols(conv_w[:, :ff]), cols(conv_w[:, ff:])
    cbl, cbg = cols(conv_b[None, :ff]), cols(conv_b[None, ff:])
    wd = w_down.reshape(nf, FFN_CHUNK, D).astype(BF16)
    return wl, wg, cwl, cwg, cbl, cbg, wd


def kernel(x, c, ctx, c_ctx, mod_w, mod_b, norm_mix_g, norm_ffn_g, ab_w_in, ab_w_out, lru_conv_w, lru_conv_b, lru_wa, lru_ba, lru_wx, lru_bx, lru_lambda, pool_w, pool_scale, cd_w_in, cd_w_out, diff_lq1, diff_lk1, diff_lq2, diff_lk2, diff_subln_g, conf_dw_w, conf_dw_b, conf_ln_g, conf_ln_b, ffn_w_up, ffn_conv_w, ffn_conv_b, ffn_w_down, final_g):
    B, L, D = x.shape
    depth = mod_w.shape[0]
    rnn_w = lru_conv_w.shape[-1]
    pool_width = pool_scale.shape[-1]
    att_w = cd_w_out.shape[1] - conf_dw_w.shape[-1]

    cv8 = jnp.zeros((8, D), F32).at[:B].set(c).at[B].set(c_ctx)
    mods = _adaln_all(cv8, mod_w, mod_b)
    final_row = final_g.reshape(1, D)

    for i in range(depth):
        j = i // 2
        need_ctx = i < depth - 1
        m = mods[i].reshape(8, N_MOD, D)
        sh1, sc1, g1, sh2, sc2, g2 = [m[:B, n][:, None, :] for n in range(N_MOD)]
        sh1c, sc1c, g1c, sh2c, sc2c, g2c = [jnp.broadcast_to(m[B, n][None, None, :], (B, 1, D)) for n in range(N_MOD)]
        gm = norm_mix_g[i].reshape(1, D)
        gf = norm_ffn_g[i].reshape(1, D)
        ffn_w = _ffn_weights(ffn_w_up[i], ffn_conv_w[i], ffn_conv_b[i], ffn_w_down[i])

        if i % 2 == 0:
            w_in = ab_w_in[j].astype(BF16)
            w_out = ab_w_out[j].astype(BF16)
            w_a, w_b = w_out[:rnn_w], w_out[rnn_w:]
            splits = ((0, rnn_w), (rnn_w, rnn_w), (2 * rnn_w, pool_width))
            wg = _lru_gate_weights(lru_wa[j], lru_wx[j])
            ba = lru_ba[j].reshape(2, 1, rnn_w)
            bx = lru_bx[j].reshape(2, 1, rnn_w)
            lam = lru_lambda[j].reshape(2, 1, rnn_w)
            pool_bd = _block_diag(pool_w[j]).astype(BF16)

            zx_c, zg_c, zp_c = _proj(ctx, gm, sh1c, sc1c, w_in, splits, "ab_in_ctx")
            hf_c, hb_c, h_fin = _lru(zx_c, lru_conv_w[j], lru_conv_b[j], wg, ba, bx, lam,
                                     jnp.zeros((2, B, 1, rnn_w), F32))
            zx, zg, zp = _proj(x, gm, sh1, sc1, w_in, splits, "ab_in")
            hf, hb, _ = _lru(zx, lru_conv_w[j], lru_conv_b[j], wg, ba, bx, lam, h_fin)
            x = _ab_out(x, hf, hb, zg, zp, w_a, w_b, pool_bd, pool_scale[j], g1)
            if need_ctx:
                ctx = _ab_out(ctx, hf_c, hb_c, zg_c, zp_c, w_a, w_b, pool_bd, pool_scale[j], g1c)
        else:
            lambda_init = 0.8 - 0.6 * math.exp(-0.3 * i)
            w_in = cd_w_in[j].astype(BF16)
            w_out = cd_w_out[j].astype(BF16)
            w_c, w_d = w_out[:att_w], w_out[att_w:]
            cos_t, sin_t = _rope_tables(L)
            lq = [a[j].reshape(1, ATT_HD) for a in (diff_lq1, diff_lk1, diff_lq2, diff_lk2)]
            sg = diff_subln_g[j].reshape(ATT_VD, 1)

            if need_ctx:
                raise NotImplementedError("an attention layer followed by further layers is not supported")
            q, k, vt, u = _cd_in(x, gm, sh1, sc1, w_in, cos_t, sin_t, att_w)
            kc, vct = _proj(ctx, gm, sh1c, sc1c, w_in, ((att_w, att_w), (2 * att_w, att_w)), "cd_kv_ctx",
                            transposed=(False, True))
            yc = _attn(q, k, vt, kc, vct, *lq, sg, lambda_init)
            x = _cd_out(x, yc, u, conf_dw_w[j], conf_dw_b[j], conf_ln_g[j], conf_ln_b[j], w_c, w_d, g1)

        x = _ffn(x, gf, sh2, sc2, g2, *ffn_w, final_row, final=(i == depth - 1))
        if need_ctx:
            ctx = _ffn(ctx, gf, sh2c, sc2c, g2c, *ffn_w, final_row, final=False)
    return x
```

```python
import functools
import math

import jax
import jax.numpy as jnp
from jax import lax
from jax.experimental import pallas as pl
from jax.experimental.pallas import tpu as pltpu

F32 = jnp.float32
BF16 = jnp.bfloat16
SDS = jax.ShapeDtypeStruct

EPS = 1e-6
N_MOD = 6
GRID_W = 64
ROPE_BASE = 10000.0
ROPE_AX = 16
ATT_HD = 64
ATT_VD = 128
ATT_VE = ATT_VD + 16
LRU_C = 8.0
POOL_WINDOWS = (2, 4, 8, 16)
CONF_K = 31
FFN_CHUNK = 256
FFN_TILE = 512
FFN_SUB = 512
FFN_ROWS = 128
FFN_PIECES = 4
ATT_TQ = 1024
ATT_TK = 512
LANES = 128
MXU_N = 256
HALO = 16
VMEM_LIMIT = 56 * 1024 * 1024


def _cparams(n_grid, flags=None):
    return pltpu.CompilerParams(dimension_semantics=("arbitrary",) * n_grid, vmem_limit_bytes=VMEM_LIMIT,
                                flags=flags)


def _tile(L):
    return min(512, L)


def _dot(a, b):
    return jnp.dot(a, b, preferred_element_type=F32)


def _modulate(xf, g, sh, sc):
    ms = jnp.mean(xf * xf, axis=-1, keepdims=True)
    return (xf * lax.rsqrt(ms + EPS) * g) * (1.0 + sc) + sh


def _shift_rows(v, o):
    n = v.shape[0]
    return pltpu.roll(v, (-o) % n, 0)


def _resident(shape):
    nd = len(shape)
    return pl.BlockSpec(shape, lambda *_: (0,) * nd, pipeline_mode=pl.Buffered(1))


def _halo_specs(T, L, width, rows=HALO):
    r = T // rows
    last = L // rows - 1
    prev = pl.BlockSpec((1, rows, width), lambda b, i: (b, jnp.maximum(i * r - 1, 0), 0))
    nxt = pl.BlockSpec((1, rows, width), lambda b, i: (b, jnp.minimum((i + 1) * r, last), 0))
    return prev, nxt


def _mod_body(c_ref, w_ref, b_ref, o_ref):
    s = c_ref[...]
    s = (s * jax.nn.sigmoid(s)).astype(BF16)
    o_ref[0] = _dot(s, w_ref[0].astype(BF16)) + b_ref[0]


def _adaln_all(cv8, mod_w, mod_b):
    depth, D, N = mod_w.shape
    tn = 1024
    return pl.pallas_call(
        _mod_body, grid=(depth, N // tn),
        in_specs=[pl.BlockSpec((8, D), lambda l, j: (0, 0)),
                  pl.BlockSpec((1, D, tn), lambda l, j: (l, 0, j)),
                  pl.BlockSpec((1, 1, tn), lambda l, j: (l, 0, j))],
        out_specs=pl.BlockSpec((1, 8, tn), lambda l, j: (l, 0, j)),
        out_shape=SDS((depth, 8, N), F32),
        compiler_params=_cparams(2), name="adaln",
    )(cv8, mod_w, mod_b.reshape(depth, 1, N))


def _store_values_t(vt_ref, z):
    ones = jnp.ones((ATT_VE - ATT_VD, z.shape[0]), BF16)
    for h in range(z.shape[1] // ATT_VD):
        vt_ref[0, h * ATT_VE:h * ATT_VE + ATT_VD, :] = z[:, h * ATT_VD:(h + 1) * ATT_VD].T.astype(BF16)
        vt_ref[0, h * ATT_VE + ATT_VD:(h + 1) * ATT_VE, :] = ones


def _values_t_rows(width):
    return width // ATT_VD * ATT_VE


def _proj_body(x_ref, g_ref, sh_ref, sc_ref, w_ref, *out_refs, splits, values_t):
    h = _modulate(x_ref[0], g_ref[...], sh_ref[0], sc_ref[0]).astype(BF16)
    for (s, wd), vt, o in zip(splits, values_t, out_refs):
        z = _dot(h, w_ref[:, s:s + wd])
        if vt:
            _store_values_t(o, z)
        else:
            o[0] = z.astype(o.dtype)


def _proj(x, g, sh, sc, w, splits, name, values_t=None):
    B, L, D = x.shape
    T = _tile(L)
    values_t = values_t or (False,) * len(splits)
    vec = pl.BlockSpec((1, 1, D), lambda b, i: (b, 0, 0))
    out_specs = [pl.BlockSpec((1, _values_t_rows(wd), T), lambda b, i: (b, 0, i)) if vt
                 else pl.BlockSpec((1, T, wd), lambda b, i: (b, i, 0)) for (_, wd), vt in zip(splits, values_t)]
    out_shape = [SDS((B, _values_t_rows(wd), L) if vt else (B, L, wd), BF16) for (_, wd), vt in zip(splits, values_t)]
    return pl.pallas_call(
        functools.partial(_proj_body, splits=splits, values_t=values_t), grid=(B, L // T),
        in_specs=[pl.BlockSpec((1, T, D), lambda b, i: (b, i, 0)), _resident((1, D)), vec, vec, _resident(w.shape)],
        out_specs=out_specs, out_shape=out_shape,
        compiler_params=_cparams(2), name=name,
    )(x, g, sh, sc, w)


def _cd_in_body(x_ref, g_ref, sh_ref, sc_ref, w_ref, cos_ref, sin_ref, q_ref, k_ref, vt_ref, u_ref, *, att_w):
    h = _modulate(x_ref[0], g_ref[...], sh_ref[0], sc_ref[0]).astype(BF16)
    cos_t = cos_ref[...]
    sin_t = sin_ref[...]
    low = (lax.broadcasted_iota(jnp.int32, cos_t.shape, 1) % (2 * ROPE_AX)) < ROPE_AX

    def rope(z):
        partner = jnp.where(low, pltpu.roll(z, LANES - ROPE_AX, 1), pltpu.roll(z, ROPE_AX, 1))
        return z * cos_t + partner * sin_t

    q_scale = math.log2(math.e) * ATT_HD ** -0.5
    for c in range(att_w // LANES):
        sl = slice(c * LANES, (c + 1) * LANES)
        q_ref[0, :, sl] = (rope(_dot(h, w_ref[:, sl])) * q_scale).astype(BF16)
        k_ref[0, :, sl] = rope(_dot(h, w_ref[:, att_w + c * LANES:att_w + (c + 1) * LANES])).astype(BF16)
    _store_values_t(vt_ref, _dot(h, w_ref[:, 2 * att_w:3 * att_w]))
    u_ref[0] = _dot(h, w_ref[:, 3 * att_w:]).astype(BF16)


def _cd_in(x, g, sh, sc, w, cos_t, sin_t, att_w):
    B, L, D = x.shape
    T = _tile(L)
    conf2 = w.shape[1] - 3 * att_w
    vec = pl.BlockSpec((1, 1, D), lambda b, i: (b, 0, 0))
    tab = pl.BlockSpec((T, LANES), lambda b, i: (i, 0))
    row = lambda wd: pl.BlockSpec((1, T, wd), lambda b, i: (b, i, 0))
    return pl.pallas_call(
        functools.partial(_cd_in_body, att_w=att_w), grid=(B, L // T),
        in_specs=[row(D), _resident((1, D)), vec, vec, _resident(w.shape), tab, tab],
        out_specs=[row(att_w), row(att_w), pl.BlockSpec((1, _values_t_rows(att_w), T), lambda b, i: (b, 0, i)),
                   row(conf2)],
        out_shape=[SDS((B, L, att_w), BF16), SDS((B, L, att_w), BF16), SDS((B, _values_t_rows(att_w), L), BF16),
                   SDS((B, L, conf2), BF16)],
        compiler_params=_cparams(2), name="cd_in",
    )(x, g, sh, sc, w, cos_t, sin_t)


def _lru_body(zf_ref, zfp_ref, zfn_ref, zb_ref, zbp_ref, zbn_ref, cw_ref, cb_ref, wg_ref, ba_ref, bx_ref, lam_ref,
              h0_ref, hf_ref, hb_ref, hfin_ref, carry_ref, z_s, xa_s, a_s, b_s, hl_s, al_s, *, T, pad_lo):
    i = pl.program_id(1)
    nt = pl.num_programs(1)
    S = T // 8
    nch = a_s.shape[1]
    K = cw_ref.shape[0]
    base = HALO - pad_lo

    @pl.when(i == 0)
    def _():
        carry_ref[...] = h0_ref[:, 0]

    tiles = ((zf_ref, zfp_ref, zfn_ref, i), (zb_ref, zbp_ref, zbn_ref, nt - 1 - i))
    for d, (z_ref, zp_ref, zn_ref, ti) in enumerate(tiles):
        keep_prev = jnp.where(ti == 0, 0.0, 1.0)
        keep_next = jnp.where(ti == nt - 1, 0.0, 1.0)
        for c in range(nch):
            sl = slice(c * LANES, (c + 1) * LANES)
            z_s[d, c, 0:HALO] = zp_ref[0, :, sl].astype(F32) * keep_prev
            z_s[d, c, HALO:HALO + T] = z_ref[0, :, sl].astype(F32)
            z_s[d, c, HALO + T:] = zn_ref[0, :, sl].astype(F32) * keep_next

            wk = [jnp.broadcast_to(cw_ref[k:k + 1, sl], (8, LANES)) for k in range(K)]
            bias = jnp.broadcast_to(cb_ref[:, sl], (8, LANES))
            win = [z_s[d, c, pl.ds(base + k, 8, stride=S), :] for k in range(K - 1)]
            for j in range(S):
                win.append(z_s[d, c, pl.ds(base + j + K - 1, 8, stride=S), :])
                xa = bias
                for k in range(K):
                    xa = xa + wk[k] * win[k]
                xa_s[d, c, j * 8:(j + 1) * 8, :] = xa
                win.pop(0)

            xc = xa_s[d, c]
            gates = _dot(xc.astype(BF16), wg_ref[d, c])
            r = 0.5 * jnp.tanh(0.5 * (gates[:, :LANES] + ba_ref[d, :, sl])) + 0.5
            gi = 0.5 * jnp.tanh(0.5 * (gates[:, LANES:] + bx_ref[d, :, sl])) + 0.5
            neg_log_a = r * (LRU_C * jax.nn.softplus(-lam_ref[d, :, sl]))
            a = jnp.exp(-neg_log_a)
            one_m_a2 = jnp.tanh(neg_log_a) * (a * a + 1.0)
            root = jnp.where(one_m_a2 > 0.0, one_m_a2 * lax.rsqrt(one_m_a2), 0.0)
            a_s[d, c] = a
            b_s[d, c] = root * (gi * xc)

    def rows_of(d, j):
        return pl.ds(pl.multiple_of((j if d == 0 else S - 1 - j) * 8, 8), 8)

    def p1(j, st):
        hs, cum = st
        nh, ncum = [], []
        for d in range(2):
            rows = rows_of(d, j)
            for c in range(nch):
                a = a_s[d, c, rows, :]
                h = a * hs[d * nch + c] + b_s[d, c, rows, :]
                ac = a * cum[d * nch + c]
                hl_s[d, c, rows, :] = h
                al_s[d, c, rows, :] = ac
                nh.append(h)
                ncum.append(ac)
        return tuple(nh), tuple(ncum)

    zeros = tuple(jnp.zeros((8, LANES), F32) for _ in range(2 * nch))
    ones = tuple(jnp.ones((8, LANES), F32) for _ in range(2 * nch))
    h_end, a_end = lax.fori_loop(0, S, p1, (zeros, ones))

    sub = lax.broadcasted_iota(jnp.int32, (8, LANES), 0)
    cmats = []
    for d in range(2):
        order = range(8) if d == 0 else range(7, -1, -1)
        for c in range(nch):
            sl = slice(c * LANES, (c + 1) * LANES)
            carry = carry_ref[d, :, sl]
            he, ae = h_end[d * nch + c], a_end[d * nch + c]
            cmat = jnp.zeros((8, LANES), F32)
            for s in order:
                cmat = jnp.where(sub == s, carry, cmat)
                carry = ae[s:s + 1, :] * carry + he[s:s + 1, :]
            carry_ref[d, :, sl] = carry
            cmats.append(cmat)
    hfin_ref[:, 0] = carry_ref[...]

    def p2(j, _):
        rows = pl.ds(pl.multiple_of(j * 8, 8), 8)
        for d in range(2):
            for c in range(nch):
                hl_s[d, c, rows, :] = hl_s[d, c, rows, :] + al_s[d, c, rows, :] * cmats[d * nch + c]
        return 0

    lax.fori_loop(0, S, p2, 0)

    out_refs = (hf_ref, hb_ref)
    for d in range(2):
        for c in range(nch):
            for s in range(8):
                out_refs[d][0, s * S:(s + 1) * S, c * LANES:(c + 1) * LANES] = (
                    hl_s[d, c, pl.ds(s, S, stride=8), :].astype(BF16))


def _lru(zx, conv_w, conv_b, wg, ba, bx, lam, h0):
    B, L, W = zx.shape
    T = _tile(L)
    nt = L // T
    nch = W // LANES
    K = conv_w.shape[0]
    fwd = pl.BlockSpec((1, T, W), lambda b, i: (b, i, 0))
    bwd = pl.BlockSpec((1, T, W), lambda b, i: (b, nt - 1 - i, 0))
    r = T // HALO
    last = L // HALO - 1
    fprev, fnext = _halo_specs(T, L, W)
    bprev = pl.BlockSpec((1, HALO, W), lambda b, i: (b, jnp.maximum((nt - 1 - i) * r - 1, 0), 0))
    bnext = pl.BlockSpec((1, HALO, W), lambda b, i: (b, jnp.minimum((nt - i) * r, last), 0))
    state = pl.BlockSpec((2, 1, 1, W), lambda b, i: (0, b, 0, 0))
    scr = pltpu.VMEM((2, nch, T, LANES), F32)
    return pl.pallas_call(
        functools.partial(_lru_body, T=T, pad_lo=K // 2), grid=(B, nt),
        in_specs=[fwd, fprev, fnext, bwd, bprev, bnext, _resident(conv_w.shape), _resident((1, W)),
                  _resident(wg.shape), _resident(ba.shape), _resident(bx.shape), _resident(lam.shape), state],
        out_specs=[fwd, bwd, state],
        out_shape=[SDS((B, L, W), BF16), SDS((B, L, W), BF16), SDS((2, B, 1, W), F32)],
        scratch_shapes=[pltpu.VMEM((2, 1, W), F32), pltpu.VMEM((2, nch, T + 2 * HALO, LANES), F32),
                        scr, scr, scr, scr, scr],
        compiler_params=_cparams(2), name="lru",
    )(zx, zx, zx, zx, zx, zx, conv_w, conv_b.reshape(1, W), wg, ba, bx, lam, h0)


def _ab_out_body(x_ref, hf_ref, hb_ref, zg_ref, zp_ref, zpp_ref, zpn_ref, wa_ref, wb_ref, pbd_ref, ps_ref, g1_ref,
                 o_ref, *, T, L):
    i = pl.program_id(1)
    nt = pl.num_programs(1)
    ya = (hf_ref[0].astype(F32) + hb_ref[0].astype(F32)) * jax.nn.gelu(zg_ref[0].astype(F32))
    y = _dot(ya.astype(BF16), wa_ref[...])

    keep_prev = jnp.where(i == 0, 0.0, 1.0)
    keep_next = jnp.where(i == nt - 1, 0.0, 1.0)
    xc = zp_ref[0].astype(F32)
    xe = jnp.concatenate([zpp_ref[0].astype(F32) * keep_prev, xc, zpn_ref[0].astype(F32) * keep_next], axis=0)
    sums = [xe + _shift_rows(xe, -1)]
    for half in (1, 2, 4):
        sums.append(_shift_rows(sums[-1], -half) + _shift_rows(sums[-1], half))
    lane = lax.broadcasted_iota(jnp.int32, xc.shape, 1)
    tpos = i * T + lax.broadcasted_iota(jnp.int32, xc.shape, 0)
    gd = xc.shape[1] // len(POOL_WINDOWS)
    mean = None
    for gi in range(len(POOL_WINDOWS) - 1, -1, -1):
        w = POOL_WINDOWS[gi]
        cnt = (jnp.minimum(tpos + (w - w // 2), L) - jnp.maximum(tpos - w // 2, 0)).astype(F32)
        m = sums[gi][HALO:HALO + T] / cnt
        mean = m if mean is None else jnp.where(lane < (gi + 1) * gd, m, mean)
    yb = _dot((mean - xc).astype(BF16), pbd_ref[...]) * ps_ref[...]
    y = y + _dot(yb.astype(BF16), wb_ref[...])
    o_ref[0] = x_ref[0] + g1_ref[0] * y


def _ab_out(x, hf, hb, zg, zp, w_a, w_b, pool_bd, pool_scale, g1):
    B, L, D = x.shape
    T = _tile(L)
    RW, PW = hf.shape[2], zp.shape[2]
    row = lambda wd: pl.BlockSpec((1, T, wd), lambda b, i: (b, i, 0))
    pprev, pnext = _halo_specs(T, L, PW)
    return pl.pallas_call(
        functools.partial(_ab_out_body, T=T, L=L), grid=(B, L // T),
        in_specs=[row(D), row(RW), row(RW), row(RW), row(PW), pprev, pnext, _resident(w_a.shape),
                  _resident(w_b.shape), _resident(pool_bd.shape), _resident((1, PW)),
                  pl.BlockSpec((1, 1, D), lambda b, i: (b, 0, 0))],
        out_specs=row(D), out_shape=SDS((B, L, D), F32),
        compiler_params=_cparams(2), name="ab_out",
    )(x, hf, hb, zg, zp, zp, zp, w_a, w_b, pool_bd, pool_scale.reshape(1, PW), g1)


def _ffn_body(x_ref, xp_ref, xn_ref, g_ref, sh_ref, sc_ref, g2_ref, wl_ref, wg_ref, cwl_ref, cwg_ref, cbl_ref,
              cbg_ref, wd_ref, fg_ref, o_ref, acc_ref, hext_s, ua_s, ub_s, acta_s, actb_s, *, T, final):
    i = pl.program_id(1)
    nt = pl.num_programs(1)
    nf = wl_ref.shape[0]
    g, sh, sc = g_ref[...], sh_ref[0], sc_ref[0]
    hp = _modulate(xp_ref[0], g, sh, sc) * jnp.where(i == 0, 0.0, 1.0)
    hn = _modulate(xn_ref[0], g, sh, sc) * jnp.where(i == nt - 1, 0.0, 1.0)
    hext_s[...] = jnp.concatenate([hp, _modulate(x_ref[0], g, sh, sc), hn], axis=0).astype(BF16)
    H = xp_ref.shape[1]

    def conv(u, cw, cb):
        mid = slice(H, u.shape[0] - H)
        return cw[0:1] * _shift_rows(u, -1)[mid] + cw[1:2] * u[mid] + cw[2:3] * _shift_rows(u, 1)[mid] + cb

    u_slots = (ua_s, ub_s)
    act_slots = (acta_s, actb_s)

    SUB = ua_s.shape[1] - 2 * H
    n_items = (T // SUB) * nf
    pieces = max(1, min(FFN_PIECES, SUB // FFN_ROWS))
    R = SUB // pieces
    RB = min(FFN_ROWS, R)
    PACK = 16

    def split(g):
        if isinstance(g, int):
            return (g // nf) * SUB, g % nf
        sub = g // nf
        return sub * SUB, g - sub * nf

    def up(g, par, p):
        base, f = split(g)
        lo = 0 if p == 0 else p * R + PACK
        hi = (p + 1) * R + PACK if p < pieces - 1 else SUB + 2 * H
        rows = pl.ds(pl.multiple_of(base + lo, PACK), hi - lo)
        u_slots[par][0, lo:hi, :] = _dot(hext_s[rows, :], wl_ref[f])
        u_slots[par][1, lo:hi, :] = _dot(hext_s[rows, :], wg_ref[f])

    def act(g, par, p):
        _, f = split(g)
        cwl, cbl, cwg, cbg = cwl_ref[f], cbl_ref[f], cwg_ref[f], cbg_ref[f]
        for r0 in range(p * R, (p + 1) * R, RB):
            ext = slice(r0, r0 + RB + 2 * H)
            lin = conv(u_slots[par][0, ext, :], cwl, cbl)
            gate = conv(u_slots[par][1, ext, :], cwg, cbg)
            act_slots[par][r0:r0 + RB, :] = (gate * jax.nn.sigmoid(gate) * lin).astype(BF16)

    def down(g, par, p):
        base, f = split(g)
        rows = pl.ds(pl.multiple_of(base + p * R, PACK), R)
        acc_ref[rows, :] += _dot(act_slots[par][p * R:(p + 1) * R, :], wd_ref[f])

    def stage(k, par):
        static = isinstance(k, int)
        for p in range(pieces):
            if not static or k >= 0:
                down(k, par, p)
            if not static or k + 2 < n_items:
                up(k + 2, par, p)
            if not static or 0 <= k + 1 < n_items:
                act(k + 1, 1 - par, p)

    acc_ref[...] = jnp.zeros_like(acc_ref)
    stage(-2, 0)
    stage(-1, 1)
    n_pairs = max(n_items - 2, 0) // 2

    def pair(jj, _):
        stage(2 * jj, 0)
        stage(2 * jj + 1, 1)
        return 0

    lax.fori_loop(0, n_pairs, pair, 0)
    for k in range(2 * n_pairs, n_items):
        stage(k, k % 2)
    out = x_ref[0] + g2_ref[0] * acc_ref[...]
    if final:
        out = out * lax.rsqrt(jnp.mean(out * out, axis=-1, keepdims=True) + EPS) * fg_ref[...]
    o_ref[0] = out


def _ffn(x, g, sh, sc, g2, wl, wg, cwl, cwg, cbl, cbg, wd, final_g, final, tile=FFN_TILE, sub=FFN_SUB):
    B, L, D = x.shape
    T = min(tile, L)
    SUB = min(sub, T)
    H = 8
    FC = wl.shape[2]
    vec = pl.BlockSpec((1, 1, D), lambda b, i: (b, 0, 0))
    xprev, xnext = _halo_specs(T, L, D, rows=H)
    row = pl.BlockSpec((1, T, D), lambda b, i: (b, i, 0))
    return pl.pallas_call(
        functools.partial(_ffn_body, T=T, final=final), grid=(B, L // T),
        in_specs=[row, xprev, xnext, _resident((1, D)), vec, vec, vec, _resident(wl.shape), _resident(wg.shape),
                  _resident(cwl.shape), _resident(cwg.shape), _resident(cbl.shape), _resident(cbg.shape),
                  _resident(wd.shape), _resident((1, D))],
        out_specs=row, out_shape=SDS((B, L, D), F32),
        scratch_shapes=[pltpu.VMEM((T, D), F32), pltpu.VMEM((T + 2 * H, D), BF16),
                        pltpu.VMEM((2, SUB + 2 * H, FC), F32), pltpu.VMEM((2, SUB + 2 * H, FC), F32),
                        pltpu.VMEM((SUB, FC), BF16), pltpu.VMEM((SUB, FC), BF16)],
        compiler_params=_cparams(2), name="ffn",
    )(x, x, x, g, sh, sc, g2, wl, wg, cwl, cwg, cbl, cbg, wd, final_g)


def _attn_body(q_ref, k_ref, vt_ref, kc_ref, vct_ref, lq1_ref, lk1_ref, lq2_ref, lk2_ref, sg_ref, o_ref,
               qm_s, sa_s, sb_s, sc_s, m_s, acc_s, *, Tk, lambda_init):
    Tq = q_ref.shape[1]
    nk = k_ref.shape[1] // Tk
    nt_dims = (((1,), (1,)), ((), ()))
    cols = [slice(c * MXU_N, (c + 1) * MXU_N) for c in range(Tq // MXU_N)]

    q = q_ref[0]
    lane = lax.broadcasted_iota(jnp.int32, q.shape, 1)
    zero = jnp.zeros_like(q)
    qm_s[0] = jnp.where(lane < ATT_HD, q, zero)
    qm_s[1] = jnp.where(lane >= ATT_HD, q, zero)
    m_s[...] = jnp.full(m_s.shape, -jnp.inf, F32)
    acc_s[...] = jnp.zeros(acc_s.shape, F32)

    def scores(kb, s_ref):
        for comp in range(2):
            for cs in cols:
                s_ref[comp, :, cs] = lax.dot_general(kb, qm_s[comp, cs, :], nt_dims, preferred_element_type=F32)

    def consume(s_ref, vtb):
        for comp in range(2):
            for cs in cols:
                s = s_ref[comp, :, cs]
                m_old = m_s[comp, :, cs]
                m_new = jnp.maximum(m_old, jnp.max(s, axis=0, keepdims=True))
                alpha = jnp.exp2(m_old - m_new)
                p = jnp.exp2(s - m_new).astype(BF16)
                acc_s[comp, :, cs] = alpha * acc_s[comp, :, cs] + _dot(vtb, p)
                m_s[comp, :, cs] = m_new

    def kblock(j):
        return k_ref[0, pl.ds(pl.multiple_of(j * Tk, Tk), Tk), :]

    def vblock(j):
        return vt_ref[0, :, pl.ds(pl.multiple_of(j * Tk, Tk), Tk)]

    scores(kc_ref[0], sc_s)
    scores(kblock(0), sa_s)
    consume(sc_s, vct_ref[0])

    def pair(jj, _):
        scores(kblock(2 * jj + 1), sb_s)
        consume(sa_s, vblock(2 * jj))
        scores(kblock(2 * jj + 2), sa_s)
        consume(sb_s, vblock(2 * jj + 1))
        return 0

    lax.fori_loop(0, nk // 2 - 1, pair, 0)
    scores(kblock(nk - 1), sb_s)
    consume(sa_s, vblock(nk - 2))
    consume(sb_s, vblock(nk - 1))

    lam = (jnp.exp(jnp.sum(lq1_ref[...] * lk1_ref[...], axis=-1, keepdims=True))
           - jnp.exp(jnp.sum(lq2_ref[...] * lk2_ref[...], axis=-1, keepdims=True)) + lambda_init)
    den = slice(ATT_VD, ATT_VD + 1)
    o = acc_s[0, :ATT_VD] * (1.0 / acc_s[0, den]) - acc_s[1, :ATT_VD] * (lam / acc_s[1, den])
    y = o * lax.rsqrt(jnp.mean(o * o, axis=0, keepdims=True) + EPS) * (sg_ref[...] * (1.0 - lambda_init))
    o_ref[0] = y.T.astype(o_ref.dtype)


def _attn(q, k, vt, kc, vct, lq1, lk1, lq2, lk2, subln_g, lambda_init, Tq=ATT_TQ, Tk=ATT_TK):
    B, L, W = q.shape
    Lc = kc.shape[1]
    H = W // ATT_VD
    Tq, Tk = min(Tq, L), min(Tk, L)
    qspec = pl.BlockSpec((1, Tq, ATT_VD), lambda b, h, i: (b, i, h))
    kspec = pl.BlockSpec((1, L, ATT_VD), lambda b, h, i: (b, 0, h))
    vspec = pl.BlockSpec((1, ATT_VE, L), lambda b, h, i: (b, h, 0))
    kcspec = pl.BlockSpec((1, Lc, ATT_VD), lambda b, h, i: (b, 0, h))
    vcspec = pl.BlockSpec((1, ATT_VE, Lc), lambda b, h, i: (b, h, 0))
    small = _resident((1, ATT_HD))
    assert (L // Tk) % 2 == 0 and Tq % MXU_N == 0
    score_slot = pltpu.VMEM((2, Tk, Tq), F32)
    return pl.pallas_call(
        functools.partial(_attn_body, Tk=Tk, lambda_init=lambda_init), grid=(B, H, L // Tq),
        in_specs=[qspec, kspec, vspec, kcspec, vcspec, small, small, small, small, _resident((ATT_VD, 1))],
        out_specs=qspec, out_shape=SDS((B, L, W), BF16),
        scratch_shapes=[pltpu.VMEM((2, Tq, ATT_VD), BF16), score_slot, score_slot, pltpu.VMEM((2, Lc, Tq), F32),
                        pltpu.VMEM((2, 1, Tq), F32), pltpu.VMEM((2, ATT_VE, Tq), F32)],
        compiler_params=_cparams(3), name="diff_attn",
    )(q, k, vt, kc, vct, lq1, lk1, lq2, lk2, subln_g)


def _cd_out_body(x_ref, yc_ref, u_ref, up_ref, un_ref, dww_ref, dwb_ref, lng_ref, lnb_ref, wc_ref, wd_ref, g1_ref,
                 o_ref, *, T):
    i = pl.program_id(1)
    nt = pl.num_programs(1)
    CW = dww_ref.shape[1]
    keep_prev = jnp.where(i == 0, 0.0, 1.0)
    keep_next = jnp.where(i == nt - 1, 0.0, 1.0)
    ue = jnp.concatenate([up_ref[0].astype(F32) * keep_prev, u_ref[0].astype(F32),
                          un_ref[0].astype(F32) * keep_next], axis=0)
    ge = ue[:, :CW] * jax.nn.sigmoid(ue[:, CW:])
    base = HALO - CONF_K // 2
    conv = dwb_ref[...]
    for r in range(8):
        shifted = _shift_rows(ge, r) if r else ge
        for k in range(CONF_K):
            off = base + k
            if off % 8 == r:
                conv = conv + dww_ref[k:k + 1, :] * shifted[off - r:off - r + T]
    mu = jnp.mean(conv, axis=-1, keepdims=True)
    cen = conv - mu
    var = jnp.mean(cen * cen, axis=-1, keepdims=True)
    ln = cen * lax.rsqrt(var + EPS) * lng_ref[...] + lnb_ref[...]
    yd = ln * jax.nn.sigmoid(ln)
    y = _dot(yc_ref[0], wc_ref[...]) + _dot(yd.astype(BF16), wd_ref[...])
    o_ref[0] = x_ref[0] + g1_ref[0] * y


def _cd_out(x, yc, u, dw_w, dw_b, ln_g, ln_b, w_c, w_d, g1):
    B, L, D = x.shape
    T = _tile(L)
    AW, UW, CW = yc.shape[2], u.shape[2], dw_w.shape[1]
    row = lambda wd: pl.BlockSpec((1, T, wd), lambda b, i: (b, i, 0))
    uprev, unext = _halo_specs(T, L, UW)
    return pl.pallas_call(
        functools.partial(_cd_out_body, T=T), grid=(B, L // T),
        in_specs=[row(D), row(AW), row(UW), uprev, unext, _resident(dw_w.shape), _resident((1, CW)),
                  _resident((1, CW)), _resident((1, CW)), _resident(w_c.shape), _resident(w_d.shape),
                  pl.BlockSpec((1, 1, D), lambda b, i: (b, 0, 0))],
        out_specs=row(D), out_shape=SDS((B, L, D), F32),
        compiler_params=_cparams(2), name="cd_out",
    )(x, yc, u, u, u, dw_w, dw_b.reshape(1, CW), ln_g.reshape(1, CW), ln_b.reshape(1, CW), w_c, w_d, g1)


def _block_diag(blocks):
    n, r, c = blocks.shape
    eye = jnp.eye(n, dtype=blocks.dtype)
    return (eye[:, None, :, None] * blocks[:, :, None, :]).reshape(n * r, n * c)


def _lru_gate_weights(wa, wx):
    ndir, heads, hd, _ = wa.shape
    per = LANES // hd

    def chunked(w):
        w = w.reshape(ndir * heads // per, per, hd, hd)
        return jax.vmap(_block_diag)(w).reshape(ndir, heads // per, LANES, LANES)

    return jnp.concatenate([chunked(wa), chunked(wx)], axis=-1).astype(BF16)


def _rope_tables(L):
    rows = L // GRID_W
    row = jnp.repeat(jnp.arange(rows), GRID_W).astype(F32)
    col = jnp.tile(jnp.arange(GRID_W), rows).astype(F32)
    inv = ROPE_BASE ** (-jnp.arange(ROPE_AX, dtype=F32) / ROPE_AX)
    ang_r = row[:, None] * inv
    ang_c = col[:, None] * inv
    ang = jnp.concatenate([ang_r, ang_r, ang_c, ang_c] * (LANES // (4 * ROPE_AX)), axis=-1)
    sign = jnp.tile(jnp.concatenate([-jnp.ones((ROPE_AX,), F32), jnp.ones((ROPE_AX,), F32)]), LANES // (2 * ROPE_AX))
    return jnp.cos(ang), jnp.sin(ang) * sign


def _ffn_weights(w_up, conv_w, conv_b, w_down):
    D, two_ff = w_up.shape
    ff = two_ff // 2
    nf = ff // FFN_CHUNK

    def cols(a):
        return a.reshape(a.shape[0], nf, FFN_CHUNK).transpose(1, 0, 2)

    wl, wg = cols(w_up[:, :ff]).astype(BF16), cols(w_up[:, ff:]).astype(BF16)
    cwl, cwg = cols(conv_w[:, :ff]), cols(conv_w[:, ff:])
    cbl, cbg = cols(conv_b[None, :ff]), cols(conv_b[None, ff:])
    wd = w_down.reshape(nf, FFN_CHUNK, D).astype(BF16)
    return wl, wg, cwl, cwg, cbl, cbg, wd


def kernel(x, c, ctx, c_ctx, mod_w, mod_b, norm_mix_g, norm_ffn_g, ab_w_in, ab_w_out, lru_conv_w, lru_conv_b, lru_wa, lru_ba, lru_wx, lru_bx, lru_lambda, pool_w, pool_scale, cd_w_in, cd_w_out, diff_lq1, diff_lk1, diff_lq2, diff_lk2, diff_subln_g, conf_dw_w, conf_dw_b, conf_ln_g, conf_ln_b, ffn_w_up, ffn_conv_w, ffn_conv_b, ffn_w_down, final_g):
    B, L, D = x.shape
    depth = mod_w.shape[0]
    rnn_w = lru_conv_w.shape[-1]
    pool_width = pool_scale.shape[-1]
    att_w = cd_w_out.shape[1] - conf_dw_w.shape[-1]

    cv8 = jnp.zeros((8, D), F32).at[:B].set(c).at[B].set(c_ctx)
    mods = _adaln_all(cv8, mod_w, mod_b)
    final_row = final_g.reshape(1, D)

    for i in range(depth):
        j = i // 2
        need_ctx = i < depth - 1
        m = mods[i].reshape(8, N_MOD, D)
        sh1, sc1, g1, sh2, sc2, g2 = [m[:B, n][:, None, :] for n in range(N_MOD)]
        sh1c, sc1c, g1c, sh2c, sc2c, g2c = [jnp.broadcast_to(m[B, n][None, None, :], (B, 1, D)) for n in range(N_MOD)]
        gm = norm_mix_g[i].reshape(1, D)
        gf = norm_ffn_g[i].reshape(1, D)
        ffn_w = _ffn_weights(ffn_w_up[i], ffn_conv_w[i], ffn_conv_b[i], ffn_w_down[i])

        if i % 2 == 0:
            w_in = ab_w_in[j].astype(BF16)
            w_out = ab_w_out[j].astype(BF16)
            w_a, w_b = w_out[:rnn_w], w_out[rnn_w:]
            splits = ((0, rnn_w), (rnn_w, rnn_w), (2 * rnn_w, pool_width))
            wg = _lru_gate_weights(lru_wa[j], lru_wx[j])
            ba = lru_ba[j].reshape(2, 1, rnn_w)
            bx = lru_bx[j].reshape(2, 1, rnn_w)
            lam = lru_lambda[j].reshape(2, 1, rnn_w)
            pool_bd = _block_diag(pool_w[j]).astype(BF16)

            zx_c, zg_c, zp_c = _proj(ctx, gm, sh1c, sc1c, w_in, splits, "ab_in_ctx")
            hf_c, hb_c, h_fin = _lru(zx_c, lru_conv_w[j], lru_conv_b[j], wg, ba, bx, lam,
                                     jnp.zeros((2, B, 1, rnn_w), F32))
            zx, zg, zp = _proj(x, gm, sh1, sc1, w_in, splits, "ab_in")
            hf, hb, _ = _lru(zx, lru_conv_w[j], lru_conv_b[j], wg, ba, bx, lam, h_fin)
            x = _ab_out(x, hf, hb, zg, zp, w_a, w_b, pool_bd, pool_scale[j], g1)
            if need_ctx:
                ctx = _ab_out(ctx, hf_c, hb_c, zg_c, zp_c, w_a, w_b, pool_bd, pool_scale[j], g1c)
        else:
            lambda_init = 0.8 - 0.6 * math.exp(-0.3 * i)
            w_in = cd_w_in[j].astype(BF16)
            w_out = cd_w_out[j].astype(BF16)
            w_c, w_d = w_out[:att_w], w_out[att_w:]
            cos_t, sin_t = _rope_tables(L)
            lq = [a[j].reshape(1, ATT_HD) for a in (diff_lq1, diff_lk1, diff_lq2, diff_lk2)]
            sg = diff_subln_g[j].reshape(ATT_VD, 1)

            if need_ctx:
                raise NotImplementedError("an attention layer followed by further layers is not supported")
            q, k, vt, u = _cd_in(x, gm, sh1, sc1, w_in, cos_t, sin_t, att_w)
            kc, vct = _proj(ctx, gm, sh1c, sc1c, w_in, ((att_w, att_w), (2 * att_w, att_w)), "cd_kv_ctx",
                            values_t=(False, True))
            yc = _attn(q, k, vt, kc, vct, *lq, sg, lambda_init)
            x = _cd_out(x, yc, u, conf_dw_w[j], conf_dw_b[j], conf_ln_g[j], conf_ln_b[j], w_c, w_d, g1)

        x = _ffn(x, gf, sh2, sc2, g2, *ffn_w, final_row, final=(i == depth - 1))
        if need_ctx:
            ctx = _ffn(ctx, gf, sh2c, sc2c, g2c, *ffn_w, final_row, final=False)
    return x
```

```python
import functools
import math

import jax
import jax.numpy as jnp
import numpy as np
from jax import lax
from jax.experimental import pallas as pl
from jax.experimental.pallas import tpu as pltpu

F32 = jnp.float32
BF16 = jnp.bfloat16
SDS = jax.ShapeDtypeStruct

EPS = 1e-6
N_MOD = 6
GRID_W = 64
ROPE_BASE = 10000.0
ROPE_AX = 16
ATT_HD = 64
ATT_VD = 128
LRU_C = 8.0
POOL_WINDOWS = (2, 4, 8, 16)
CONF_K = 31
FFN_CHUNK = 256
FFN_TILE = 512
ATT_TQ = 2048
ATT_TK = 512
LANES = 128
MXU_N = 256
HALO = 16
VMEM_LIMIT = 56 * 1024 * 1024


def _cparams(n_grid):
    return pltpu.CompilerParams(dimension_semantics=("arbitrary",) * n_grid, vmem_limit_bytes=VMEM_LIMIT)


def _tile(L):
    return min(512, L)


def _dot(a, b):
    return jnp.dot(a, b, preferred_element_type=F32)


def _modulate(xf, g, sh, sc):
    ms = jnp.mean(xf * xf, axis=-1, keepdims=True)
    return (xf * lax.rsqrt(ms + EPS) * g) * (1.0 + sc) + sh


def _shift_rows(v, o):
    n = v.shape[0]
    return pltpu.roll(v, (-o) % n, 0)


def _resident(shape):
    nd = len(shape)
    return pl.BlockSpec(shape, lambda *_: (0,) * nd, pipeline_mode=pl.Buffered(1))


def _halo_specs(T, L, width, rows=HALO):
    r = T // rows
    last = L // rows - 1
    prev = pl.BlockSpec((1, rows, width), lambda b, i: (b, jnp.maximum(i * r - 1, 0), 0))
    nxt = pl.BlockSpec((1, rows, width), lambda b, i: (b, jnp.minimum((i + 1) * r, last), 0))
    return prev, nxt


def _mod_body(c_ref, w_ref, b_ref, o_ref):
    s = c_ref[...]
    s = (s * jax.nn.sigmoid(s)).astype(BF16)
    o_ref[0] = _dot(s, w_ref[0].astype(BF16)) + b_ref[0]


def _adaln_all(cv8, mod_w, mod_b):
    depth, D, N = mod_w.shape
    tn = 1024
    return pl.pallas_call(
        _mod_body, grid=(depth, N // tn),
        in_specs=[pl.BlockSpec((8, D), lambda l, j: (0, 0)),
                  pl.BlockSpec((1, D, tn), lambda l, j: (l, 0, j)),
                  pl.BlockSpec((1, 1, tn), lambda l, j: (l, 0, j))],
        out_specs=pl.BlockSpec((1, 8, tn), lambda l, j: (l, 0, j)),
        out_shape=SDS((depth, 8, N), F32),
        compiler_params=_cparams(2), name="adaln",
    )(cv8, mod_w, mod_b.reshape(depth, 1, N))


def _proj_body(x_ref, g_ref, sh_ref, sc_ref, w_ref, *out_refs, splits, transposed):
    h = _modulate(x_ref[0], g_ref[...], sh_ref[0], sc_ref[0]).astype(BF16)
    for (s, wd), tr, o in zip(splits, transposed, out_refs):
        z = _dot(h, w_ref[:, s:s + wd])
        o[0] = (z.T if tr else z).astype(o.dtype)


def _proj(x, g, sh, sc, w, splits, name, transposed=None):
    B, L, D = x.shape
    T = _tile(L)
    transposed = transposed or (False,) * len(splits)
    vec = pl.BlockSpec((1, 1, D), lambda b, i: (b, 0, 0))
    out_specs = [pl.BlockSpec((1, wd, T), lambda b, i: (b, 0, i)) if tr else pl.BlockSpec((1, T, wd), lambda b, i: (b, i, 0))
                 for (_, wd), tr in zip(splits, transposed)]
    out_shape = [SDS((B, wd, L) if tr else (B, L, wd), BF16) for (_, wd), tr in zip(splits, transposed)]
    return pl.pallas_call(
        functools.partial(_proj_body, splits=splits, transposed=transposed), grid=(B, L // T),
        in_specs=[pl.BlockSpec((1, T, D), lambda b, i: (b, i, 0)), _resident((1, D)), vec, vec, _resident(w.shape)],
        out_specs=out_specs, out_shape=out_shape,
        compiler_params=_cparams(2), name=name,
    )(x, g, sh, sc, w)


def _cd_in_body(x_ref, g_ref, sh_ref, sc_ref, w_ref, cos_ref, sin_ref, q_ref, k_ref, vt_ref, u_ref, *, att_w):
    h = _modulate(x_ref[0], g_ref[...], sh_ref[0], sc_ref[0]).astype(BF16)
    cos_t = cos_ref[...]
    sin_t = sin_ref[...]
    low = (lax.broadcasted_iota(jnp.int32, cos_t.shape, 1) % (2 * ROPE_AX)) < ROPE_AX

    def rope(z):
        partner = jnp.where(low, pltpu.roll(z, LANES - ROPE_AX, 1), pltpu.roll(z, ROPE_AX, 1))
        return z * cos_t + partner * sin_t

    q_scale = math.log2(math.e) * ATT_HD ** -0.5
    for c in range(att_w // MXU_N):
        zq = _dot(h, w_ref[:, c * MXU_N:(c + 1) * MXU_N])
        zk = _dot(h, w_ref[:, att_w + c * MXU_N:att_w + (c + 1) * MXU_N])
        for half in range(MXU_N // LANES):
            src = slice(half * LANES, (half + 1) * LANES)
            dst = slice(c * MXU_N + half * LANES, c * MXU_N + (half + 1) * LANES)
            q_ref[0, :, dst] = (rope(zq[:, src]) * q_scale).astype(BF16)
            k_ref[0, :, dst] = rope(zk[:, src]).astype(BF16)
    vt_ref[0] = _dot(h, w_ref[:, 2 * att_w:3 * att_w]).T.astype(BF16)
    u_ref[0] = _dot(h, w_ref[:, 3 * att_w:]).astype(BF16)


def _cd_in(x, g, sh, sc, w, cos_t, sin_t, att_w):
    B, L, D = x.shape
    T = _tile(L)
    conf2 = w.shape[1] - 3 * att_w
    vec = pl.BlockSpec((1, 1, D), lambda b, i: (b, 0, 0))
    tab = pl.BlockSpec((T, LANES), lambda b, i: (i, 0))
    row = lambda wd: pl.BlockSpec((1, T, wd), lambda b, i: (b, i, 0))
    return pl.pallas_call(
        functools.partial(_cd_in_body, att_w=att_w), grid=(B, L // T),
        in_specs=[row(D), _resident((1, D)), vec, vec, _resident(w.shape), tab, tab],
        out_specs=[row(att_w), row(att_w), pl.BlockSpec((1, att_w, T), lambda b, i: (b, 0, i)), row(conf2)],
        out_shape=[SDS((B, L, att_w), BF16), SDS((B, L, att_w), BF16), SDS((B, att_w, L), BF16),
                   SDS((B, L, conf2), BF16)],
        compiler_params=_cparams(2), name="cd_in",
    )(x, g, sh, sc, w, cos_t, sin_t)


def _lru_body(zf_ref, zfp_ref, zfn_ref, zb_ref, zbp_ref, zbn_ref, cw_ref, cb_ref, wg_ref, ba_ref, bx_ref, lam_ref,
              h0_ref, hf_ref, hb_ref, hfin_ref, carry_ref, z_s, xa_s, a_s, b_s, hl_s, al_s, *, T, pad_lo):
    i = pl.program_id(1)
    nt = pl.num_programs(1)
    S = T // 8
    nch = a_s.shape[1]
    K = cw_ref.shape[0]
    base = HALO - pad_lo

    @pl.when(i == 0)
    def _():
        carry_ref[...] = h0_ref[:, 0]

    tiles = ((zf_ref, zfp_ref, zfn_ref, i), (zb_ref, zbp_ref, zbn_ref, nt - 1 - i))
    for d, (z_ref, zp_ref, zn_ref, ti) in enumerate(tiles):
        keep_prev = jnp.where(ti == 0, 0.0, 1.0)
        keep_next = jnp.where(ti == nt - 1, 0.0, 1.0)
        for c in range(nch):
            sl = slice(c * LANES, (c + 1) * LANES)
            z_s[d, c, 0:HALO] = zp_ref[0, :, sl].astype(F32) * keep_prev
            z_s[d, c, HALO:HALO + T] = z_ref[0, :, sl].astype(F32)
            z_s[d, c, HALO + T:] = zn_ref[0, :, sl].astype(F32) * keep_next

            wk = [jnp.broadcast_to(cw_ref[k:k + 1, sl], (8, LANES)) for k in range(K)]
            bias = jnp.broadcast_to(cb_ref[:, sl], (8, LANES))
            win = [z_s[d, c, pl.ds(base + k, 8, stride=S), :] for k in range(K - 1)]
            for j in range(S):
                win.append(z_s[d, c, pl.ds(base + j + K - 1, 8, stride=S), :])
                xa = bias
                for k in range(K):
                    xa = xa + wk[k] * win[k]
                xa_s[d, c, j * 8:(j + 1) * 8, :] = xa
                win.pop(0)

            xc = xa_s[d, c]
            gates = _dot(xc.astype(BF16), wg_ref[d, c])
            r = 0.5 * jnp.tanh(0.5 * (gates[:, :LANES] + ba_ref[d, :, sl])) + 0.5
            gi = 0.5 * jnp.tanh(0.5 * (gates[:, LANES:] + bx_ref[d, :, sl])) + 0.5
            neg_log_a = r * (LRU_C * jax.nn.softplus(-lam_ref[d, :, sl]))
            a = jnp.exp(-neg_log_a)
            one_m_a2 = jnp.tanh(neg_log_a) * (a * a + 1.0)
            root = jnp.where(one_m_a2 > 0.0, one_m_a2 * lax.rsqrt(one_m_a2), 0.0)
            a_s[d, c] = a
            b_s[d, c] = root * (gi * xc)

    def rows_of(d, j):
        return pl.ds(pl.multiple_of((j if d == 0 else S - 1 - j) * 8, 8), 8)

    def p1(j, st):
        hs, cum = st
        nh, ncum = [], []
        for d in range(2):
            rows = rows_of(d, j)
            for c in range(nch):
                a = a_s[d, c, rows, :]
                h = a * hs[d * nch + c] + b_s[d, c, rows, :]
                ac = a * cum[d * nch + c]
                hl_s[d, c, rows, :] = h
                al_s[d, c, rows, :] = ac
                nh.append(h)
                ncum.append(ac)
        return tuple(nh), tuple(ncum)

    zeros = tuple(jnp.zeros((8, LANES), F32) for _ in range(2 * nch))
    ones = tuple(jnp.ones((8, LANES), F32) for _ in range(2 * nch))
    h_end, a_end = lax.fori_loop(0, S, p1, (zeros, ones))

    sub = lax.broadcasted_iota(jnp.int32, (8, LANES), 0)
    cmats = []
    for d in range(2):
        order = range(8) if d == 0 else range(7, -1, -1)
        for c in range(nch):
            sl = slice(c * LANES, (c + 1) * LANES)
            carry = carry_ref[d, :, sl]
            he, ae = h_end[d * nch + c], a_end[d * nch + c]
            cmat = jnp.zeros((8, LANES), F32)
            for s in order:
                cmat = jnp.where(sub == s, carry, cmat)
                carry = ae[s:s + 1, :] * carry + he[s:s + 1, :]
            carry_ref[d, :, sl] = carry
            cmats.append(cmat)
    hfin_ref[:, 0] = carry_ref[...]

    def p2(j, _):
        rows = pl.ds(pl.multiple_of(j * 8, 8), 8)
        for d in range(2):
            for c in range(nch):
                hl_s[d, c, rows, :] = hl_s[d, c, rows, :] + al_s[d, c, rows, :] * cmats[d * nch + c]
        return 0

    lax.fori_loop(0, S, p2, 0)

    out_refs = (hf_ref, hb_ref)
    for d in range(2):
        for c in range(nch):
            for s in range(8):
                out_refs[d][0, s * S:(s + 1) * S, c * LANES:(c + 1) * LANES] = (
                    hl_s[d, c, pl.ds(s, S, stride=8), :].astype(BF16))


def _lru(zx, conv_w, conv_b, wg, ba, bx, lam, h0):
    B, L, W = zx.shape
    T = _tile(L)
    nt = L // T
    nch = W // LANES
    K = conv_w.shape[0]
    fwd = pl.BlockSpec((1, T, W), lambda b, i: (b, i, 0))
    bwd = pl.BlockSpec((1, T, W), lambda b, i: (b, nt - 1 - i, 0))
    r = T // HALO
    last = L // HALO - 1
    fprev, fnext = _halo_specs(T, L, W)
    bprev = pl.BlockSpec((1, HALO, W), lambda b, i: (b, jnp.maximum((nt - 1 - i) * r - 1, 0), 0))
    bnext = pl.BlockSpec((1, HALO, W), lambda b, i: (b, jnp.minimum((nt - i) * r, last), 0))
    state = pl.BlockSpec((2, 1, 1, W), lambda b, i: (0, b, 0, 0))
    scr = pltpu.VMEM((2, nch, T, LANES), F32)
    return pl.pallas_call(
        functools.partial(_lru_body, T=T, pad_lo=K // 2), grid=(B, nt),
        in_specs=[fwd, fprev, fnext, bwd, bprev, bnext, _resident(conv_w.shape), _resident((1, W)),
                  _resident(wg.shape), _resident(ba.shape), _resident(bx.shape), _resident(lam.shape), state],
        out_specs=[fwd, bwd, state],
        out_shape=[SDS((B, L, W), BF16), SDS((B, L, W), BF16), SDS((2, B, 1, W), F32)],
        scratch_shapes=[pltpu.VMEM((2, 1, W), F32), pltpu.VMEM((2, nch, T + 2 * HALO, LANES), F32),
                        scr, scr, scr, scr, scr],
        compiler_params=_cparams(2), name="lru",
    )(zx, zx, zx, zx, zx, zx, conv_w, conv_b.reshape(1, W), wg, ba, bx, lam, h0)


def _ab_out_body(x_ref, hf_ref, hb_ref, zg_ref, zp_ref, zpp_ref, zpn_ref, wa_ref, wb_ref, pbd_ref, ps_ref, g1_ref,
                 o_ref, *, T, L):
    i = pl.program_id(1)
    nt = pl.num_programs(1)
    ya = (hf_ref[0].astype(F32) + hb_ref[0].astype(F32)) * jax.nn.gelu(zg_ref[0].astype(F32))
    y = _dot(ya.astype(BF16), wa_ref[...])

    keep_prev = jnp.where(i == 0, 0.0, 1.0)
    keep_next = jnp.where(i == nt - 1, 0.0, 1.0)
    xc = zp_ref[0].astype(F32)
    xe = jnp.concatenate([zpp_ref[0].astype(F32) * keep_prev, xc, zpn_ref[0].astype(F32) * keep_next], axis=0)
    sums = [xe + _shift_rows(xe, -1)]
    for half in (1, 2, 4):
        sums.append(_shift_rows(sums[-1], -half) + _shift_rows(sums[-1], half))
    lane = lax.broadcasted_iota(jnp.int32, xc.shape, 1)
    tpos = i * T + lax.broadcasted_iota(jnp.int32, xc.shape, 0)
    gd = xc.shape[1] // len(POOL_WINDOWS)
    mean = None
    for gi in range(len(POOL_WINDOWS) - 1, -1, -1):
        w = POOL_WINDOWS[gi]
        cnt = (jnp.minimum(tpos + (w - w // 2), L) - jnp.maximum(tpos - w // 2, 0)).astype(F32)
        m = sums[gi][HALO:HALO + T] / cnt
        mean = m if mean is None else jnp.where(lane < (gi + 1) * gd, m, mean)
    yb = _dot((mean - xc).astype(BF16), pbd_ref[...]) * ps_ref[...]
    y = y + _dot(yb.astype(BF16), wb_ref[...])
    o_ref[0] = x_ref[0] + g1_ref[0] * y


def _ab_out(x, hf, hb, zg, zp, w_a, w_b, pool_bd, pool_scale, g1):
    B, L, D = x.shape
    T = _tile(L)
    RW, PW = hf.shape[2], zp.shape[2]
    row = lambda wd: pl.BlockSpec((1, T, wd), lambda b, i: (b, i, 0))
    pprev, pnext = _halo_specs(T, L, PW)
    return pl.pallas_call(
        functools.partial(_ab_out_body, T=T, L=L), grid=(B, L // T),
        in_specs=[row(D), row(RW), row(RW), row(RW), row(PW), pprev, pnext, _resident(w_a.shape),
                  _resident(w_b.shape), _resident(pool_bd.shape), _resident((1, PW)),
                  pl.BlockSpec((1, 1, D), lambda b, i: (b, 0, 0))],
        out_specs=row(D), out_shape=SDS((B, L, D), F32),
        compiler_params=_cparams(2), name="ab_out",
    )(x, hf, hb, zg, zp, zp, zp, w_a, w_b, pool_bd, pool_scale.reshape(1, PW), g1)


def _ffn_body(x_ref, xp_ref, xn_ref, g_ref, sh_ref, sc_ref, g2_ref, wl_ref, wg_ref, cwl_ref, cwg_ref, cbl_ref,
              cbg_ref, wd_ref, fg_ref, o_ref, acc_ref, hext_s, ua_s, ub_s, *, T, final):
    i = pl.program_id(1)
    nt = pl.num_programs(1)
    nf = wl_ref.shape[0]
    g, sh, sc = g_ref[...], sh_ref[0], sc_ref[0]
    hp = _modulate(xp_ref[0], g, sh, sc) * jnp.where(i == 0, 0.0, 1.0)
    hn = _modulate(xn_ref[0], g, sh, sc) * jnp.where(i == nt - 1, 0.0, 1.0)
    hext_s[...] = jnp.concatenate([hp, _modulate(x_ref[0], g, sh, sc), hn], axis=0).astype(BF16)
    H = xp_ref.shape[1]

    def conv(u, cw, cb):
        return (cw[0:1] * _shift_rows(u, -1) + cw[1:2] * u + cw[2:3] * _shift_rows(u, 1))[H:H + T] + cb

    def up(f, u_s):
        u_s[0] = _dot(hext_s[...], wl_ref[f])
        u_s[1] = _dot(hext_s[...], wg_ref[f])

    def down(f, u_s):
        lin = conv(u_s[0], cwl_ref[f], cbl_ref[f])
        gate = conv(u_s[1], cwg_ref[f], cbg_ref[f])
        act = (gate * jax.nn.sigmoid(gate) * lin).astype(BF16)
        acc_ref[...] += _dot(act, wd_ref[f])

    acc_ref[...] = jnp.zeros_like(acc_ref)
    up(0, ua_s)

    def pair(jj, _):
        up(2 * jj + 1, ub_s)
        down(2 * jj, ua_s)
        up(2 * jj + 2, ua_s)
        down(2 * jj + 1, ub_s)
        return 0

    lax.fori_loop(0, (nf - 1) // 2, pair, 0)
    if nf % 2 == 0:
        up(nf - 1, ub_s)
        down(nf - 2, ua_s)
        down(nf - 1, ub_s)
    else:
        down(nf - 1, ua_s)
    out = x_ref[0] + g2_ref[0] * acc_ref[...]
    if final:
        out = out * lax.rsqrt(jnp.mean(out * out, axis=-1, keepdims=True) + EPS) * fg_ref[...]
    o_ref[0] = out


def _ffn(x, g, sh, sc, g2, wl, wg, cwl, cwg, cbl, cbg, wd, final_g, final, tile=FFN_TILE):
    B, L, D = x.shape
    T = min(tile, L)
    H = 8
    vec = pl.BlockSpec((1, 1, D), lambda b, i: (b, 0, 0))
    xprev, xnext = _halo_specs(T, L, D, rows=H)
    row = pl.BlockSpec((1, T, D), lambda b, i: (b, i, 0))
    return pl.pallas_call(
        functools.partial(_ffn_body, T=T, final=final), grid=(B, L // T),
        in_specs=[row, xprev, xnext, _resident((1, D)), vec, vec, vec, _resident(wl.shape), _resident(wg.shape),
                  _resident(cwl.shape), _resident(cwg.shape), _resident(cbl.shape), _resident(cbg.shape),
                  _resident(wd.shape), _resident((1, D))],
        out_specs=row, out_shape=SDS((B, L, D), F32),
        scratch_shapes=[pltpu.VMEM((T, D), F32), pltpu.VMEM((T + 2 * H, D), BF16),
                        pltpu.VMEM((2, T + 2 * H, wl.shape[2]), F32), pltpu.VMEM((2, T + 2 * H, wl.shape[2]), F32)],
        compiler_params=_cparams(2), name="ffn",
    )(x, x, x, g, sh, sc, g2, wl, wg, cwl, cwg, cbl, cbg, wd, final_g)


def _attn_body(q_ref, k_ref, vt_ref, kc_ref, vct_ref, lq1_ref, lk1_ref, lq2_ref, lk2_ref, sg_ref, o_ref,
               qm_s, sa_s, sb_s, sc_s, m_s, l_s, acc_s, *, Tk, lambda_init):
    Tq = q_ref.shape[1]
    nk = k_ref.shape[1] // Tk
    nt_dims = (((1,), (1,)), ((), ()))
    cols = [slice(c * MXU_N, (c + 1) * MXU_N) for c in range(Tq // MXU_N)]

    q = q_ref[0]
    lane = lax.broadcasted_iota(jnp.int32, q.shape, 1)
    zero = jnp.zeros_like(q)
    qm_s[0] = jnp.where(lane < ATT_HD, q, zero)
    qm_s[1] = jnp.where(lane >= ATT_HD, q, zero)
    m_s[...] = jnp.full(m_s.shape, -jnp.inf, F32)
    l_s[...] = jnp.zeros(l_s.shape, F32)
    acc_s[...] = jnp.zeros(acc_s.shape, F32)

    def scores(kb, s_ref):
        for comp in range(2):
            for cs in cols:
                s_ref[comp, :, cs] = lax.dot_general(kb, qm_s[comp, cs, :], nt_dims, preferred_element_type=F32)

    def consume(s_ref, vtb):
        for comp in range(2):
            for cs in cols:
                s = s_ref[comp, :, cs]
                m_old = m_s[comp, :, cs]
                m_new = jnp.maximum(m_old, jnp.max(s, axis=0, keepdims=True))
                alpha = jnp.exp2(m_old - m_new)
                p = jnp.exp2(s - m_new)
                l_s[comp, :, cs] = alpha * l_s[comp, :, cs] + jnp.sum(p, axis=0, keepdims=True)
                acc_s[comp, :, cs] = alpha * acc_s[comp, :, cs] + _dot(vtb, p.astype(BF16))
                m_s[comp, :, cs] = m_new

    def kblock(j):
        return k_ref[0, pl.ds(pl.multiple_of(j * Tk, Tk), Tk), :]

    def vblock(j):
        return vt_ref[0, :, pl.ds(pl.multiple_of(j * Tk, Tk), Tk)]

    scores(kc_ref[0], sc_s)
    scores(kblock(0), sa_s)
    consume(sc_s, vct_ref[0])

    def pair(jj, _):
        scores(kblock(2 * jj + 1), sb_s)
        consume(sa_s, vblock(2 * jj))
        scores(kblock(2 * jj + 2), sa_s)
        consume(sb_s, vblock(2 * jj + 1))
        return 0

    lax.fori_loop(0, nk // 2 - 1, pair, 0)
    scores(kblock(nk - 1), sb_s)
    consume(sa_s, vblock(nk - 2))
    consume(sb_s, vblock(nk - 1))

    lam = (jnp.exp(jnp.sum(lq1_ref[...] * lk1_ref[...], axis=-1, keepdims=True))
           - jnp.exp(jnp.sum(lq2_ref[...] * lk2_ref[...], axis=-1, keepdims=True)) + lambda_init)
    o = acc_s[0] * (1.0 / l_s[0]) - acc_s[1] * (lam / l_s[1])
    y = o * lax.rsqrt(jnp.mean(o * o, axis=0, keepdims=True) + EPS) * (sg_ref[...] * (1.0 - lambda_init))
    o_ref[0] = y.T.astype(o_ref.dtype)


def _attn(q, k, vt, kc, vct, lq1, lk1, lq2, lk2, subln_g, lambda_init, Tq=ATT_TQ, Tk=ATT_TK):
    B, L, W = q.shape
    Lc = kc.shape[1]
    H = W // ATT_VD
    Tq, Tk = min(Tq, L), min(Tk, L)
    qspec = pl.BlockSpec((1, Tq, ATT_VD), lambda b, h, i: (b, i, h))
    kspec = pl.BlockSpec((1, L, ATT_VD), lambda b, h, i: (b, 0, h))
    vspec = pl.BlockSpec((1, ATT_VD, L), lambda b, h, i: (b, h, 0))
    kcspec = pl.BlockSpec((1, Lc, ATT_VD), lambda b, h, i: (b, 0, h))
    vcspec = pl.BlockSpec((1, ATT_VD, Lc), lambda b, h, i: (b, h, 0))
    small = _resident((1, ATT_HD))
    assert (L // Tk) % 2 == 0 and Tq % MXU_N == 0
    score_slot = pltpu.VMEM((2, Tk, Tq), F32)
    return pl.pallas_call(
        functools.partial(_attn_body, Tk=Tk, lambda_init=lambda_init), grid=(B, H, L // Tq),
        in_specs=[qspec, kspec, vspec, kcspec, vcspec, small, small, small, small, _resident((ATT_VD, 1))],
        out_specs=qspec, out_shape=SDS((B, L, W), BF16),
        scratch_shapes=[pltpu.VMEM((2, Tq, ATT_VD), BF16), score_slot, score_slot, pltpu.VMEM((2, Lc, Tq), F32),
                        pltpu.VMEM((2, 1, Tq), F32), pltpu.VMEM((2, 1, Tq), F32), pltpu.VMEM((2, ATT_VD, Tq), F32)],
        compiler_params=_cparams(3), name="diff_attn",
    )(q, k, vt, kc, vct, lq1, lk1, lq2, lk2, subln_g)


def _cd_out_body(x_ref, yc_ref, u_ref, up_ref, un_ref, dww_ref, dwb_ref, lng_ref, lnb_ref, wc_ref, wd_ref, g1_ref,
                 o_ref, *, T):
    i = pl.program_id(1)
    nt = pl.num_programs(1)
    CW = dww_ref.shape[1]
    keep_prev = jnp.where(i == 0, 0.0, 1.0)
    keep_next = jnp.where(i == nt - 1, 0.0, 1.0)
    ue = jnp.concatenate([up_ref[0].astype(F32) * keep_prev, u_ref[0].astype(F32),
                          un_ref[0].astype(F32) * keep_next], axis=0)
    ge = ue[:, :CW] * jax.nn.sigmoid(ue[:, CW:])
    base = HALO - CONF_K // 2
    conv = dwb_ref[...]
    for r in range(8):
        shifted = _shift_rows(ge, r) if r else ge
        for k in range(CONF_K):
            off = base + k
            if off % 8 == r:
                conv = conv + dww_ref[k:k + 1, :] * shifted[off - r:off - r + T]
    mu = jnp.mean(conv, axis=-1, keepdims=True)
    cen = conv - mu
    var = jnp.mean(cen * cen, axis=-1, keepdims=True)
    ln = cen * lax.rsqrt(var + EPS) * lng_ref[...] + lnb_ref[...]
    yd = ln * jax.nn.sigmoid(ln)
    y = _dot(yc_ref[0], wc_ref[...]) + _dot(yd.astype(BF16), wd_ref[...])
    o_ref[0] = x_ref[0] + g1_ref[0] * y


def _cd_out(x, yc, u, dw_w, dw_b, ln_g, ln_b, w_c, w_d, g1):
    B, L, D = x.shape
    T = _tile(L)
    AW, UW, CW = yc.shape[2], u.shape[2], dw_w.shape[1]
    row = lambda wd: pl.BlockSpec((1, T, wd), lambda b, i: (b, i, 0))
    uprev, unext = _halo_specs(T, L, UW)
    return pl.pallas_call(
        functools.partial(_cd_out_body, T=T), grid=(B, L // T),
        in_specs=[row(D), row(AW), row(UW), uprev, unext, _resident(dw_w.shape), _resident((1, CW)),
                  _resident((1, CW)), _resident((1, CW)), _resident(w_c.shape), _resident(w_d.shape),
                  pl.BlockSpec((1, 1, D), lambda b, i: (b, 0, 0))],
        out_specs=row(D), out_shape=SDS((B, L, D), F32),
        compiler_params=_cparams(2), name="cd_out",
    )(x, yc, u, u, u, dw_w, dw_b.reshape(1, CW), ln_g.reshape(1, CW), ln_b.reshape(1, CW), w_c, w_d, g1)


def _block_diag(blocks):
    n, r, c = blocks.shape
    eye = jnp.eye(n, dtype=blocks.dtype)
    return (eye[:, None, :, None] * blocks[:, :, None, :]).reshape(n * r, n * c)


def _lru_gate_weights(wa, wx):
    ndir, heads, hd, _ = wa.shape
    per = LANES // hd

    def chunked(w):
        w = w.reshape(ndir * heads // per, per, hd, hd)
        return jax.vmap(_block_diag)(w).reshape(ndir, heads // per, LANES, LANES)

    return jnp.concatenate([chunked(wa), chunked(wx)], axis=-1).astype(BF16)


def _rope_tables(L):
    rows = L // GRID_W
    row = np.repeat(np.arange(rows), GRID_W).astype(np.float32)
    col = np.tile(np.arange(GRID_W), rows).astype(np.float32)
    inv = (ROPE_BASE ** (-np.arange(ROPE_AX, dtype=np.float32) / ROPE_AX)).astype(np.float32)
    ang_r = row[:, None] * inv
    ang_c = col[:, None] * inv
    ang = np.concatenate([ang_r, ang_r, ang_c, ang_c] * (LANES // (4 * ROPE_AX)), axis=-1)
    sign = np.tile(np.concatenate([-np.ones(ROPE_AX, np.float32), np.ones(ROPE_AX, np.float32)]), LANES // (2 * ROPE_AX))
    return jnp.asarray(np.cos(ang), F32), jnp.asarray(np.sin(ang) * sign, F32)


def _ffn_weights(w_up, conv_w, conv_b, w_down):
    D, two_ff = w_up.shape
    ff = two_ff // 2
    nf = ff // FFN_CHUNK

    def cols(a):
        return a.reshape(a.shape[0], nf, FFN_CHUNK).transpose(1, 0, 2)

    wl, wg = cols(w_up[:, :ff]).astype(BF16), cols(w_up[:, ff:]).astype(BF16)
    cwl, cwg = cols(conv_w[:, :ff]), cols(conv_w[:, ff:])
    cbl, cbg = cols(conv_b[None, :ff]), cols(conv_b[None, ff:])
    wd = w_down.reshape(nf, FFN_CHUNK, D).astype(BF16)
    return wl, wg, cwl, cwg, cbl, cbg, wd


def kernel(x, c, ctx, c_ctx, mod_w, mod_b, norm_mix_g, norm_ffn_g, ab_w_in, ab_w_out, lru_conv_w, lru_conv_b, lru_wa, lru_ba, lru_wx, lru_bx, lru_lambda, pool_w, pool_scale, cd_w_in, cd_w_out, diff_lq1, diff_lk1, diff_lq2, diff_lk2, diff_subln_g, conf_dw_w, conf_dw_b, conf_ln_g, conf_ln_b, ffn_w_up, ffn_conv_w, ffn_conv_b, ffn_w_down, final_g):
    B, L, D = x.shape
    depth = mod_w.shape[0]
    rnn_w = lru_conv_w.shape[-1]
    pool_width = pool_scale.shape[-1]
    att_w = cd_w_out.shape[1] - conf_dw_w.shape[-1]

    cv8 = jnp.zeros((8, D), F32).at[:B].set(c).at[B].set(c_ctx)
    mods = _adaln_all(cv8, mod_w, mod_b)
    final_row = final_g.reshape(1, D)

    for i in range(depth):
        j = i // 2
        need_ctx = i < depth - 1
        m = mods[i].reshape(8, N_MOD, D)
        sh1, sc1, g1, sh2, sc2, g2 = [m[:B, n][:, None, :] for n in range(N_MOD)]
        sh1c, sc1c, g1c, sh2c, sc2c, g2c = [jnp.broadcast_to(m[B, n][None, None, :], (B, 1, D)) for n in range(N_MOD)]
        gm = norm_mix_g[i].reshape(1, D)
        gf = norm_ffn_g[i].reshape(1, D)
        ffn_w = _ffn_weights(ffn_w_up[i], ffn_conv_w[i], ffn_conv_b[i], ffn_w_down[i])

        if i % 2 == 0:
            w_in = ab_w_in[j].astype(BF16)
            w_out = ab_w_out[j].astype(BF16)
            w_a, w_b = w_out[:rnn_w], w_out[rnn_w:]
            splits = ((0, rnn_w), (rnn_w, rnn_w), (2 * rnn_w, pool_width))
            wg = _lru_gate_weights(lru_wa[j], lru_wx[j])
            ba = lru_ba[j].reshape(2, 1, rnn_w)
            bx = lru_bx[j].reshape(2, 1, rnn_w)
            lam = lru_lambda[j].reshape(2, 1, rnn_w)
            pool_bd = _block_diag(pool_w[j]).astype(BF16)

            zx_c, zg_c, zp_c = _proj(ctx, gm, sh1c, sc1c, w_in, splits, "ab_in_ctx")
            hf_c, hb_c, h_fin = _lru(zx_c, lru_conv_w[j], lru_conv_b[j], wg, ba, bx, lam,
                                     jnp.zeros((2, B, 1, rnn_w), F32))
            zx, zg, zp = _proj(x, gm, sh1, sc1, w_in, splits, "ab_in")
            hf, hb, _ = _lru(zx, lru_conv_w[j], lru_conv_b[j], wg, ba, bx, lam, h_fin)
            x = _ab_out(x, hf, hb, zg, zp, w_a, w_b, pool_bd, pool_scale[j], g1)
            if need_ctx:
                ctx = _ab_out(ctx, hf_c, hb_c, zg_c, zp_c, w_a, w_b, pool_bd, pool_scale[j], g1c)
        else:
            lambda_init = 0.8 - 0.6 * math.exp(-0.3 * i)
            w_in = cd_w_in[j].astype(BF16)
            w_out = cd_w_out[j].astype(BF16)
            w_c, w_d = w_out[:att_w], w_out[att_w:]
            cos_t, sin_t = _rope_tables(L)
            lq = [a[j].reshape(1, ATT_HD) for a in (diff_lq1, diff_lk1, diff_lq2, diff_lk2)]
            sg = diff_subln_g[j].reshape(ATT_VD, 1)

            if need_ctx:
                raise NotImplementedError("an attention layer followed by further layers is not supported")
            q, k, vt, u = _cd_in(x, gm, sh1, sc1, w_in, cos_t, sin_t, att_w)
            kc, vct = _proj(ctx, gm, sh1c, sc1c, w_in, ((att_w, att_w), (2 * att_w, att_w)), "cd_kv_ctx",
                            transposed=(False, True))
            yc = _attn(q, k, vt, kc, vct, *lq, sg, lambda_init)
            x = _cd_out(x, yc, u, conf_dw_w[j], conf_dw_b[j], conf_ln_g[j], conf_ln_b[j], w_c, w_d, g1)

        x = _ffn(x, gf, sh2, sc2, g2, *ffn_w, final_row, final=(i == depth - 1))
        if need_ctx:
            ctx = _ffn(ctx, gf, sh2c, sc2c, g2c, *ffn_w, final_row, final=False)
    return x
```

```python
import functools
import math

import jax
import jax.numpy as jnp
import numpy as np
from jax import lax
from jax.experimental import pallas as pl
from jax.experimental.pallas import tpu as pltpu

F32 = jnp.float32
BF16 = jnp.bfloat16
SDS = jax.ShapeDtypeStruct

EPS = 1e-6
N_MOD = 6
GRID_W = 64
ROPE_BASE = 10000.0
ROPE_AX = 16
ATT_HD = 64
ATT_VD = 128
LRU_C = 8.0
POOL_WINDOWS = (2, 4, 8, 16)
CONF_K = 31
FFN_CHUNK = 256
FFN_TILE = 1024
ROW_TILE = 1024
LRU_TILE = 512
ATT_TQ = 2048
ATT_TK = 512
LANES = 128
MXU_N = 256
HALO = 16
VMEM_LIMIT = 56 * 1024 * 1024


def _cparams(n_grid):
    return pltpu.CompilerParams(dimension_semantics=("arbitrary",) * n_grid, vmem_limit_bytes=VMEM_LIMIT)


def _tile(L):
    return min(ROW_TILE, L)


def _dot(a, b):
    return jnp.dot(a, b, preferred_element_type=F32)


def _modulate(xf, g, sh, sc):
    ms = jnp.mean(xf * xf, axis=-1, keepdims=True)
    return (xf * lax.rsqrt(ms + EPS) * g) * (1.0 + sc) + sh


def _shift_rows(v, o):
    n = v.shape[0]
    return pltpu.roll(v, (-o) % n, 0)


def _resident(shape):
    nd = len(shape)
    return pl.BlockSpec(shape, lambda *_: (0,) * nd, pipeline_mode=pl.Buffered(1))


def _halo_specs(T, L, width, rows=HALO):
    r = T // rows
    last = L // rows - 1
    prev = pl.BlockSpec((1, rows, width), lambda b, i: (b, jnp.maximum(i * r - 1, 0), 0))
    nxt = pl.BlockSpec((1, rows, width), lambda b, i: (b, jnp.minimum((i + 1) * r, last), 0))
    return prev, nxt


def _mod_body(c_ref, w_ref, b_ref, o_ref):
    s = c_ref[...]
    s = (s * jax.nn.sigmoid(s)).astype(BF16)
    o_ref[0] = _dot(s, w_ref[0].astype(BF16)) + b_ref[0]


def _adaln_all(cv8, mod_w, mod_b):
    depth, D, N = mod_w.shape
    tn = 1024
    return pl.pallas_call(
        _mod_body, grid=(depth, N // tn),
        in_specs=[pl.BlockSpec((8, D), lambda l, j: (0, 0)),
                  pl.BlockSpec((1, D, tn), lambda l, j: (l, 0, j)),
                  pl.BlockSpec((1, 1, tn), lambda l, j: (l, 0, j))],
        out_specs=pl.BlockSpec((1, 8, tn), lambda l, j: (l, 0, j)),
        out_shape=SDS((depth, 8, N), F32),
        compiler_params=_cparams(2), name="adaln",
    )(cv8, mod_w, mod_b.reshape(depth, 1, N))


def _proj_body(x_ref, g_ref, sh_ref, sc_ref, w_ref, *out_refs, splits, transposed):
    h = _modulate(x_ref[0], g_ref[...], sh_ref[0], sc_ref[0]).astype(BF16)
    for (s, wd), tr, o in zip(splits, transposed, out_refs):
        z = _dot(h, w_ref[:, s:s + wd])
        o[0] = (z.T if tr else z).astype(o.dtype)


def _proj(x, g, sh, sc, w, splits, name, transposed=None):
    B, L, D = x.shape
    T = _tile(L)
    transposed = transposed or (False,) * len(splits)
    vec = pl.BlockSpec((1, 1, D), lambda b, i: (b, 0, 0))
    out_specs = [pl.BlockSpec((1, wd, T), lambda b, i: (b, 0, i)) if tr else pl.BlockSpec((1, T, wd), lambda b, i: (b, i, 0))
                 for (_, wd), tr in zip(splits, transposed)]
    out_shape = [SDS((B, wd, L) if tr else (B, L, wd), BF16) for (_, wd), tr in zip(splits, transposed)]
    return pl.pallas_call(
        functools.partial(_proj_body, splits=splits, transposed=transposed), grid=(B, L // T),
        in_specs=[pl.BlockSpec((1, T, D), lambda b, i: (b, i, 0)), _resident((1, D)), vec, vec, _resident(w.shape)],
        out_specs=out_specs, out_shape=out_shape,
        compiler_params=_cparams(2), name=name,
    )(x, g, sh, sc, w)


def _cd_in_body(x_ref, g_ref, sh_ref, sc_ref, w_ref, cos_ref, sin_ref, q_ref, k_ref, vt_ref, u_ref, *, att_w):
    h = _modulate(x_ref[0], g_ref[...], sh_ref[0], sc_ref[0]).astype(BF16)
    cos_t = cos_ref[...]
    sin_t = sin_ref[...]
    low = (lax.broadcasted_iota(jnp.int32, cos_t.shape, 1) % (2 * ROPE_AX)) < ROPE_AX

    def rope(z):
        partner = jnp.where(low, pltpu.roll(z, LANES - ROPE_AX, 1), pltpu.roll(z, ROPE_AX, 1))
        return z * cos_t + partner * sin_t

    q_scale = math.log2(math.e) * ATT_HD ** -0.5
    for c in range(att_w // MXU_N):
        zq = _dot(h, w_ref[:, c * MXU_N:(c + 1) * MXU_N])
        zk = _dot(h, w_ref[:, att_w + c * MXU_N:att_w + (c + 1) * MXU_N])
        for half in range(MXU_N // LANES):
            src = slice(half * LANES, (half + 1) * LANES)
            dst = slice(c * MXU_N + half * LANES, c * MXU_N + (half + 1) * LANES)
            q_ref[0, :, dst] = (rope(zq[:, src]) * q_scale).astype(BF16)
            k_ref[0, :, dst] = rope(zk[:, src]).astype(BF16)
    vt_ref[0] = _dot(h, w_ref[:, 2 * att_w:3 * att_w]).T.astype(BF16)
    u_ref[0] = _dot(h, w_ref[:, 3 * att_w:]).astype(BF16)


def _cd_in(x, g, sh, sc, w, cos_t, sin_t, att_w):
    B, L, D = x.shape
    T = _tile(L)
    conf2 = w.shape[1] - 3 * att_w
    vec = pl.BlockSpec((1, 1, D), lambda b, i: (b, 0, 0))
    tab = pl.BlockSpec((T, LANES), lambda b, i: (i, 0))
    row = lambda wd: pl.BlockSpec((1, T, wd), lambda b, i: (b, i, 0))
    return pl.pallas_call(
        functools.partial(_cd_in_body, att_w=att_w), grid=(B, L // T),
        in_specs=[row(D), _resident((1, D)), vec, vec, _resident(w.shape), tab, tab],
        out_specs=[row(att_w), row(att_w), pl.BlockSpec((1, att_w, T), lambda b, i: (b, 0, i)), row(conf2)],
        out_shape=[SDS((B, L, att_w), BF16), SDS((B, L, att_w), BF16), SDS((B, att_w, L), BF16),
                   SDS((B, L, conf2), BF16)],
        compiler_params=_cparams(2), name="cd_in",
    )(x, g, sh, sc, w, cos_t, sin_t)


def _lru_body(zf_ref, zfp_ref, zfn_ref, zb_ref, zbp_ref, zbn_ref, cw_ref, cb_ref, wg_ref, ba_ref, bx_ref, lam_ref,
              h0_ref, hf_ref, hb_ref, hfin_ref, carry_ref, z_s, xa_s, a_s, b_s, hl_s, al_s, *, T, pad_lo):
    i = pl.program_id(1)
    nt = pl.num_programs(1)
    S = T // 8
    nch = a_s.shape[1]
    K = cw_ref.shape[0]
    base = HALO - pad_lo

    @pl.when(i == 0)
    def _():
        carry_ref[...] = h0_ref[:, 0]

    tiles = ((zf_ref, zfp_ref, zfn_ref, i), (zb_ref, zbp_ref, zbn_ref, nt - 1 - i))
    for d, (z_ref, zp_ref, zn_ref, ti) in enumerate(tiles):
        keep_prev = jnp.where(ti == 0, 0.0, 1.0)
        keep_next = jnp.where(ti == nt - 1, 0.0, 1.0)
        for c in range(nch):
            sl = slice(c * LANES, (c + 1) * LANES)
            z_s[d, c, 0:HALO] = zp_ref[0, :, sl].astype(F32) * keep_prev
            z_s[d, c, HALO:HALO + T] = z_ref[0, :, sl].astype(F32)
            z_s[d, c, HALO + T:] = zn_ref[0, :, sl].astype(F32) * keep_next

            wk = [jnp.broadcast_to(cw_ref[k:k + 1, sl], (8, LANES)) for k in range(K)]
            bias = jnp.broadcast_to(cb_ref[:, sl], (8, LANES))
            win = [z_s[d, c, pl.ds(base + k, 8, stride=S), :] for k in range(K - 1)]
            for j in range(S):
                win.append(z_s[d, c, pl.ds(base + j + K - 1, 8, stride=S), :])
                xa = bias
                for k in range(K):
                    xa = xa + wk[k] * win[k]
                xa_s[d, c, j * 8:(j + 1) * 8, :] = xa
                win.pop(0)

            xc = xa_s[d, c]
            gates = _dot(xc.astype(BF16), wg_ref[d, c])
            r = 0.5 * jnp.tanh(0.5 * (gates[:, :LANES] + ba_ref[d, :, sl])) + 0.5
            gi = 0.5 * jnp.tanh(0.5 * (gates[:, LANES:] + bx_ref[d, :, sl])) + 0.5
            neg_log_a = r * (LRU_C * jax.nn.softplus(-lam_ref[d, :, sl]))
            a = jnp.exp(-neg_log_a)
            one_m_a2 = jnp.tanh(neg_log_a) * (a * a + 1.0)
            root = jnp.where(one_m_a2 > 0.0, one_m_a2 * lax.rsqrt(one_m_a2), 0.0)
            a_s[d, c] = a
            b_s[d, c] = root * (gi * xc)

    def rows_of(d, j):
        return pl.ds(pl.multiple_of((j if d == 0 else S - 1 - j) * 8, 8), 8)

    def p1(j, st):
        hs, cum = st
        nh, ncum = [], []
        for d in range(2):
            rows = rows_of(d, j)
            for c in range(nch):
                a = a_s[d, c, rows, :]
                h = a * hs[d * nch + c] + b_s[d, c, rows, :]
                ac = a * cum[d * nch + c]
                hl_s[d, c, rows, :] = h
                al_s[d, c, rows, :] = ac
                nh.append(h)
                ncum.append(ac)
        return tuple(nh), tuple(ncum)

    zeros = tuple(jnp.zeros((8, LANES), F32) for _ in range(2 * nch))
    ones = tuple(jnp.ones((8, LANES), F32) for _ in range(2 * nch))
    h_end, a_end = lax.fori_loop(0, S, p1, (zeros, ones))

    sub = lax.broadcasted_iota(jnp.int32, (8, LANES), 0)
    cmats = []
    for d in range(2):
        order = range(8) if d == 0 else range(7, -1, -1)
        for c in range(nch):
            sl = slice(c * LANES, (c + 1) * LANES)
            carry = carry_ref[d, :, sl]
            he, ae = h_end[d * nch + c], a_end[d * nch + c]
            cmat = jnp.zeros((8, LANES), F32)
            for s in order:
                cmat = jnp.where(sub == s, carry, cmat)
                carry = ae[s:s + 1, :] * carry + he[s:s + 1, :]
            carry_ref[d, :, sl] = carry
            cmats.append(cmat)
    hfin_ref[:, 0] = carry_ref[...]

    def p2(j, _):
        rows = pl.ds(pl.multiple_of(j * 8, 8), 8)
        for d in range(2):
            for c in range(nch):
                hl_s[d, c, rows, :] = hl_s[d, c, rows, :] + al_s[d, c, rows, :] * cmats[d * nch + c]
        return 0

    lax.fori_loop(0, S, p2, 0)

    out_refs = (hf_ref, hb_ref)
    for d in range(2):
        for c in range(nch):
            for s in range(8):
                out_refs[d][0, s * S:(s + 1) * S, c * LANES:(c + 1) * LANES] = (
                    hl_s[d, c, pl.ds(s, S, stride=8), :].astype(BF16))


def _lru(zx, conv_w, conv_b, wg, ba, bx, lam, h0):
    B, L, W = zx.shape
    T = min(LRU_TILE, L)
    nt = L // T
    nch = W // LANES
    K = conv_w.shape[0]
    fwd = pl.BlockSpec((1, T, W), lambda b, i: (b, i, 0))
    bwd = pl.BlockSpec((1, T, W), lambda b, i: (b, nt - 1 - i, 0))
    r = T // HALO
    last = L // HALO - 1
    fprev, fnext = _halo_specs(T, L, W)
    bprev = pl.BlockSpec((1, HALO, W), lambda b, i: (b, jnp.maximum((nt - 1 - i) * r - 1, 0), 0))
    bnext = pl.BlockSpec((1, HALO, W), lambda b, i: (b, jnp.minimum((nt - i) * r, last), 0))
    state = pl.BlockSpec((2, 1, 1, W), lambda b, i: (0, b, 0, 0))
    scr = pltpu.VMEM((2, nch, T, LANES), F32)
    return pl.pallas_call(
        functools.partial(_lru_body, T=T, pad_lo=K // 2), grid=(B, nt),
        in_specs=[fwd, fprev, fnext, bwd, bprev, bnext, _resident(conv_w.shape), _resident((1, W)),
                  _resident(wg.shape), _resident(ba.shape), _resident(bx.shape), _resident(lam.shape), state],
        out_specs=[fwd, bwd, state],
        out_shape=[SDS((B, L, W), BF16), SDS((B, L, W), BF16), SDS((2, B, 1, W), F32)],
        scratch_shapes=[pltpu.VMEM((2, 1, W), F32), pltpu.VMEM((2, nch, T + 2 * HALO, LANES), F32),
                        scr, scr, scr, scr, scr],
        compiler_params=_cparams(2), name="lru",
    )(zx, zx, zx, zx, zx, zx, conv_w, conv_b.reshape(1, W), wg, ba, bx, lam, h0)


def _ab_out_body(x_ref, hf_ref, hb_ref, zg_ref, zp_ref, zpp_ref, zpn_ref, wa_ref, wb_ref, pbd_ref, ps_ref, g1_ref,
                 o_ref, *, T, L):
    i = pl.program_id(1)
    nt = pl.num_programs(1)
    ya = (hf_ref[0].astype(F32) + hb_ref[0].astype(F32)) * jax.nn.gelu(zg_ref[0].astype(F32))
    y = _dot(ya.astype(BF16), wa_ref[...])

    keep_prev = jnp.where(i == 0, 0.0, 1.0)
    keep_next = jnp.where(i == nt - 1, 0.0, 1.0)
    xc = zp_ref[0].astype(F32)
    xe = jnp.concatenate([zpp_ref[0].astype(F32) * keep_prev, xc, zpn_ref[0].astype(F32) * keep_next], axis=0)
    sums = [xe + _shift_rows(xe, -1)]
    for half in (1, 2, 4):
        sums.append(_shift_rows(sums[-1], -half) + _shift_rows(sums[-1], half))
    lane = lax.broadcasted_iota(jnp.int32, xc.shape, 1)
    tpos = i * T + lax.broadcasted_iota(jnp.int32, xc.shape, 0)
    gd = xc.shape[1] // len(POOL_WINDOWS)
    mean = None
    for gi in range(len(POOL_WINDOWS) - 1, -1, -1):
        w = POOL_WINDOWS[gi]
        cnt = (jnp.minimum(tpos + (w - w // 2), L) - jnp.maximum(tpos - w // 2, 0)).astype(F32)
        m = sums[gi][HALO:HALO + T] / cnt
        mean = m if mean is None else jnp.where(lane < (gi + 1) * gd, m, mean)
    yb = _dot((mean - xc).astype(BF16), pbd_ref[...]) * ps_ref[...]
    y = y + _dot(yb.astype(BF16), wb_ref[...])
    o_ref[0] = x_ref[0] + g1_ref[0] * y


def _ab_out(x, hf, hb, zg, zp, w_a, w_b, pool_bd, pool_scale, g1):
    B, L, D = x.shape
    T = _tile(L)
    RW, PW = hf.shape[2], zp.shape[2]
    row = lambda wd: pl.BlockSpec((1, T, wd), lambda b, i: (b, i, 0))
    pprev, pnext = _halo_specs(T, L, PW)
    return pl.pallas_call(
        functools.partial(_ab_out_body, T=T, L=L), grid=(B, L // T),
        in_specs=[row(D), row(RW), row(RW), row(RW), row(PW), pprev, pnext, _resident(w_a.shape),
                  _resident(w_b.shape), _resident(pool_bd.shape), _resident((1, PW)),
                  pl.BlockSpec((1, 1, D), lambda b, i: (b, 0, 0))],
        out_specs=row(D), out_shape=SDS((B, L, D), F32),
        compiler_params=_cparams(2), name="ab_out",
    )(x, hf, hb, zg, zp, zp, zp, w_a, w_b, pool_bd, pool_scale.reshape(1, PW), g1)


def _ffn_body(x_ref, xp_ref, xn_ref, g_ref, sh_ref, sc_ref, g2_ref, wl_ref, wg_ref, cwl_ref, cwg_ref, cbl_ref,
              cbg_ref, wd_ref, fg_ref, o_ref, acc_ref, hext_s, ua_s, ub_s, *, T, final):
    i = pl.program_id(1)
    nt = pl.num_programs(1)
    nf = wl_ref.shape[0]
    g, sh, sc = g_ref[...], sh_ref[0], sc_ref[0]
    hp = _modulate(xp_ref[0], g, sh, sc) * jnp.where(i == 0, 0.0, 1.0)
    hn = _modulate(xn_ref[0], g, sh, sc) * jnp.where(i == nt - 1, 0.0, 1.0)
    hext_s[...] = jnp.concatenate([hp, _modulate(x_ref[0], g, sh, sc), hn], axis=0).astype(BF16)
    H = xp_ref.shape[1]

    def conv(u, cw, cb):
        return (cw[0:1] * _shift_rows(u, -1) + cw[1:2] * u + cw[2:3] * _shift_rows(u, 1))[H:H + T] + cb

    def up(f, u_s):
        u_s[0] = _dot(hext_s[...], wl_ref[f])
        u_s[1] = _dot(hext_s[...], wg_ref[f])

    def down(f, u_s):
        lin = conv(u_s[0], cwl_ref[f], cbl_ref[f])
        gate = conv(u_s[1], cwg_ref[f], cbg_ref[f])
        act = (gate * jax.nn.sigmoid(gate) * lin).astype(BF16)
        acc_ref[...] += _dot(act, wd_ref[f])

    acc_ref[...] = jnp.zeros_like(acc_ref)
    up(0, ua_s)

    def pair(jj, _):
        up(2 * jj + 1, ub_s)
        down(2 * jj, ua_s)
        up(2 * jj + 2, ua_s)
        down(2 * jj + 1, ub_s)
        return 0

    lax.fori_loop(0, (nf - 1) // 2, pair, 0)
    if nf % 2 == 0:
        up(nf - 1, ub_s)
        down(nf - 2, ua_s)
        down(nf - 1, ub_s)
    else:
        down(nf - 1, ua_s)
    out = x_ref[0] + g2_ref[0] * acc_ref[...]
    if final:
        out = out * lax.rsqrt(jnp.mean(out * out, axis=-1, keepdims=True) + EPS) * fg_ref[...]
    o_ref[0] = out


def _ffn(x, g, sh, sc, g2, wl, wg, cwl, cwg, cbl, cbg, wd, final_g, final, tile=FFN_TILE):
    B, L, D = x.shape
    T = min(tile, L)
    H = 8
    vec = pl.BlockSpec((1, 1, D), lambda b, i: (b, 0, 0))
    xprev, xnext = _halo_specs(T, L, D, rows=H)
    row = pl.BlockSpec((1, T, D), lambda b, i: (b, i, 0))
    return pl.pallas_call(
        functools.partial(_ffn_body, T=T, final=final), grid=(B, L // T),
        in_specs=[row, xprev, xnext, _resident((1, D)), vec, vec, vec, _resident(wl.shape), _resident(wg.shape),
                  _resident(cwl.shape), _resident(cwg.shape), _resident(cbl.shape), _resident(cbg.shape),
                  _resident(wd.shape), _resident((1, D))],
        out_specs=row, out_shape=SDS((B, L, D), F32),
        scratch_shapes=[pltpu.VMEM((T, D), F32), pltpu.VMEM((T + 2 * H, D), BF16),
                        pltpu.VMEM((2, T + 2 * H, wl.shape[2]), F32), pltpu.VMEM((2, T + 2 * H, wl.shape[2]), F32)],
        compiler_params=_cparams(2), name="ffn",
    )(x, x, x, g, sh, sc, g2, wl, wg, cwl, cwg, cbl, cbg, wd, final_g)


def _attn_body(q_ref, k_ref, vt_ref, kc_ref, vct_ref, lq1_ref, lk1_ref, lq2_ref, lk2_ref, sg_ref, o_ref,
               qm_s, sa_s, sb_s, sc_s, m_s, l_s, acc_s, *, Tk, lambda_init):
    Tq = q_ref.shape[1]
    nk = k_ref.shape[1] // Tk
    nt_dims = (((1,), (1,)), ((), ()))
    cols = [slice(c * MXU_N, (c + 1) * MXU_N) for c in range(Tq // MXU_N)]

    q = q_ref[0]
    lane = lax.broadcasted_iota(jnp.int32, q.shape, 1)
    zero = jnp.zeros_like(q)
    qm_s[0] = jnp.where(lane < ATT_HD, q, zero)
    qm_s[1] = jnp.where(lane >= ATT_HD, q, zero)
    m_s[...] = jnp.full(m_s.shape, -jnp.inf, F32)
    l_s[...] = jnp.zeros(l_s.shape, F32)
    acc_s[...] = jnp.zeros(acc_s.shape, F32)

    def scores(kb, s_ref):
        for comp in range(2):
            for cs in cols:
                s_ref[comp, :, cs] = lax.dot_general(kb, qm_s[comp, cs, :], nt_dims, preferred_element_type=F32)

    def consume(s_ref, vtb):
        for comp in range(2):
            for cs in cols:
                s = s_ref[comp, :, cs]
                m_old = m_s[comp, :, cs]
                m_new = jnp.maximum(m_old, jnp.max(s, axis=0, keepdims=True))
                alpha = jnp.exp2(m_old - m_new)
                p = jnp.exp2(s - m_new)
                l_s[comp, :, cs] = alpha * l_s[comp, :, cs] + jnp.sum(p, axis=0, keepdims=True)
                acc_s[comp, :, cs] = alpha * acc_s[comp, :, cs] + _dot(vtb, p.astype(BF16))
                m_s[comp, :, cs] = m_new

    def kblock(j):
        return k_ref[0, pl.ds(pl.multiple_of(j * Tk, Tk), Tk), :]

    def vblock(j):
        return vt_ref[0, :, pl.ds(pl.multiple_of(j * Tk, Tk), Tk)]

    scores(kc_ref[0], sc_s)
    scores(kblock(0), sa_s)
    consume(sc_s, vct_ref[0])

    def pair(jj, _):
        scores(kblock(2 * jj + 1), sb_s)
        consume(sa_s, vblock(2 * jj))
        scores(kblock(2 * jj + 2), sa_s)
        consume(sb_s, vblock(2 * jj + 1))
        return 0

    lax.fori_loop(0, nk // 2 - 1, pair, 0)
    scores(kblock(nk - 1), sb_s)
    consume(sa_s, vblock(nk - 2))
    consume(sb_s, vblock(nk - 1))

    lam = (jnp.exp(jnp.sum(lq1_ref[...] * lk1_ref[...], axis=-1, keepdims=True))
           - jnp.exp(jnp.sum(lq2_ref[...] * lk2_ref[...], axis=-1, keepdims=True)) + lambda_init)
    o = acc_s[0] * (1.0 / l_s[0]) - acc_s[1] * (lam / l_s[1])
    y = o * lax.rsqrt(jnp.mean(o * o, axis=0, keepdims=True) + EPS) * (sg_ref[...] * (1.0 - lambda_init))
    o_ref[0] = y.T.astype(o_ref.dtype)


def _attn(q, k, vt, kc, vct, lq1, lk1, lq2, lk2, subln_g, lambda_init, Tq=ATT_TQ, Tk=ATT_TK):
    B, L, W = q.shape
    Lc = kc.shape[1]
    H = W // ATT_VD
    Tq, Tk = min(Tq, L), min(Tk, L)
    qspec = pl.BlockSpec((1, Tq, ATT_VD), lambda b, h, i: (b, i, h))
    kspec = pl.BlockSpec((1, L, ATT_VD), lambda b, h, i: (b, 0, h))
    vspec = pl.BlockSpec((1, ATT_VD, L), lambda b, h, i: (b, h, 0))
    kcspec = pl.BlockSpec((1, Lc, ATT_VD), lambda b, h, i: (b, 0, h))
    vcspec = pl.BlockSpec((1, ATT_VD, Lc), lambda b, h, i: (b, h, 0))
    small = _resident((1, ATT_HD))
    assert (L // Tk) % 2 == 0 and Tq % MXU_N == 0
    score_slot = pltpu.VMEM((2, Tk, Tq), F32)
    return pl.pallas_call(
        functools.partial(_attn_body, Tk=Tk, lambda_init=lambda_init), grid=(B, H, L // Tq),
        in_specs=[qspec, kspec, vspec, kcspec, vcspec, small, small, small, small, _resident((ATT_VD, 1))],
        out_specs=qspec, out_shape=SDS((B, L, W), BF16),
        scratch_shapes=[pltpu.VMEM((2, Tq, ATT_VD), BF16), score_slot, score_slot, pltpu.VMEM((2, Lc, Tq), F32),
                        pltpu.VMEM((2, 1, Tq), F32), pltpu.VMEM((2, 1, Tq), F32), pltpu.VMEM((2, ATT_VD, Tq), F32)],
        compiler_params=_cparams(3), name="diff_attn",
    )(q, k, vt, kc, vct, lq1, lk1, lq2, lk2, subln_g)


def _cd_out_body(x_ref, yc_ref, u_ref, up_ref, un_ref, dww_ref, dwb_ref, lng_ref, lnb_ref, wc_ref, wd_ref, g1_ref,
                 o_ref, *, T):
    i = pl.program_id(1)
    nt = pl.num_programs(1)
    CW = dww_ref.shape[1]
    keep_prev = jnp.where(i == 0, 0.0, 1.0)
    keep_next = jnp.where(i == nt - 1, 0.0, 1.0)
    ue = jnp.concatenate([up_ref[0].astype(F32) * keep_prev, u_ref[0].astype(F32),
                          un_ref[0].astype(F32) * keep_next], axis=0)
    ge = ue[:, :CW] * jax.nn.sigmoid(ue[:, CW:])
    base = HALO - CONF_K // 2
    conv = dwb_ref[...]
    for r in range(8):
        shifted = _shift_rows(ge, r) if r else ge
        for k in range(CONF_K):
            off = base + k
            if off % 8 == r:
                conv = conv + dww_ref[k:k + 1, :] * shifted[off - r:off - r + T]
    mu = jnp.mean(conv, axis=-1, keepdims=True)
    cen = conv - mu
    var = jnp.mean(cen * cen, axis=-1, keepdims=True)
    ln = cen * lax.rsqrt(var + EPS) * lng_ref[...] + lnb_ref[...]
    yd = ln * jax.nn.sigmoid(ln)
    y = _dot(yc_ref[0], wc_ref[...]) + _dot(yd.astype(BF16), wd_ref[...])
    o_ref[0] = x_ref[0] + g1_ref[0] * y


def _cd_out(x, yc, u, dw_w, dw_b, ln_g, ln_b, w_c, w_d, g1):
    B, L, D = x.shape
    T = _tile(L)
    AW, UW, CW = yc.shape[2], u.shape[2], dw_w.shape[1]
    row = lambda wd: pl.BlockSpec((1, T, wd), lambda b, i: (b, i, 0))
    uprev, unext = _halo_specs(T, L, UW)
    return pl.pallas_call(
        functools.partial(_cd_out_body, T=T), grid=(B, L // T),
        in_specs=[row(D), row(AW), row(UW), uprev, unext, _resident(dw_w.shape), _resident((1, CW)),
                  _resident((1, CW)), _resident((1, CW)), _resident(w_c.shape), _resident(w_d.shape),
                  pl.BlockSpec((1, 1, D), lambda b, i: (b, 0, 0))],
        out_specs=row(D), out_shape=SDS((B, L, D), F32),
        compiler_params=_cparams(2), name="cd_out",
    )(x, yc, u, u, u, dw_w, dw_b.reshape(1, CW), ln_g.reshape(1, CW), ln_b.reshape(1, CW), w_c, w_d, g1)


def _block_diag(blocks):
    n, r, c = blocks.shape
    eye = jnp.eye(n, dtype=blocks.dtype)
    return (eye[:, None, :, None] * blocks[:, :, None, :]).reshape(n * r, n * c)


def _lru_gate_weights(wa, wx):
    ndir, heads, hd, _ = wa.shape
    per = LANES // hd

    def chunked(w):
        w = w.reshape(ndir * heads // per, per, hd, hd)
        return jax.vmap(_block_diag)(w).reshape(ndir, heads // per, LANES, LANES)

    return jnp.concatenate([chunked(wa), chunked(wx)], axis=-1).astype(BF16)


def _rope_tables(L):
    rows = L // GRID_W
    row = np.repeat(np.arange(rows), GRID_W).astype(np.float32)
    col = np.tile(np.arange(GRID_W), rows).astype(np.float32)
    inv = (ROPE_BASE ** (-np.arange(ROPE_AX, dtype=np.float32) / ROPE_AX)).astype(np.float32)
    ang_r = row[:, None] * inv
    ang_c = col[:, None] * inv
    ang = np.concatenate([ang_r, ang_r, ang_c, ang_c] * (LANES // (4 * ROPE_AX)), axis=-1)
    sign = np.tile(np.concatenate([-np.ones(ROPE_AX, np.float32), np.ones(ROPE_AX, np.float32)]), LANES // (2 * ROPE_AX))
    return jnp.asarray(np.cos(ang), F32), jnp.asarray(np.sin(ang) * sign, F32)


def _ffn_weights(w_up, conv_w, conv_b, w_down):
    D, two_ff = w_up.shape
    ff = two_ff // 2
    nf = ff // FFN_CHUNK

    def cols(a):
        return a.reshape(a.shape[0], nf, FFN_CHUNK).transpose(1, 0, 2)

    wl, wg = cols(w_up[:, :ff]).astype(BF16), cols(w_up[:, ff:]).astype(BF16)
    cwl, cwg = cols(conv_w[:, :ff]), cols(conv_w[:, ff:])
    cbl, cbg = cols(conv_b[None, :ff]), cols(conv_b[None, ff:])
    wd = w_down.reshape(nf, FFN_CHUNK, D).astype(BF16)
    return wl, wg, cwl, cwg, cbl, cbg, wd


def kernel(x, c, ctx, c_ctx, mod_w, mod_b, norm_mix_g, norm_ffn_g, ab_w_in, ab_w_out, lru_conv_w, lru_conv_b, lru_wa, lru_ba, lru_wx, lru_bx, lru_lambda, pool_w, pool_scale, cd_w_in, cd_w_out, diff_lq1, diff_lk1, diff_lq2, diff_lk2, diff_subln_g, conf_dw_w, conf_dw_b, conf_ln_g, conf_ln_b, ffn_w_up, ffn_conv_w, ffn_conv_b, ffn_w_down, final_g):
    B, L, D = x.shape
    depth = mod_w.shape[0]
    rnn_w = lru_conv_w.shape[-1]
    pool_width = pool_scale.shape[-1]
    att_w = cd_w_out.shape[1] - conf_dw_w.shape[-1]

    cv8 = jnp.zeros((8, D), F32).at[:B].set(c).at[B].set(c_ctx)
    mods = _adaln_all(cv8, mod_w, mod_b)
    final_row = final_g.reshape(1, D)

    for i in range(depth):
        j = i // 2
        need_ctx = i < depth - 1
        m = mods[i].reshape(8, N_MOD, D)
        sh1, sc1, g1, sh2, sc2, g2 = [m[:B, n][:, None, :] for n in range(N_MOD)]
        sh1c, sc1c, g1c, sh2c, sc2c, g2c = [jnp.broadcast_to(m[B, n][None, None, :], (B, 1, D)) for n in range(N_MOD)]
        gm = norm_mix_g[i].reshape(1, D)
        gf = norm_ffn_g[i].reshape(1, D)
        ffn_w = _ffn_weights(ffn_w_up[i], ffn_conv_w[i], ffn_conv_b[i], ffn_w_down[i])

        if i % 2 == 0:
            w_in = ab_w_in[j].astype(BF16)
            w_out = ab_w_out[j].astype(BF16)
            w_a, w_b = w_out[:rnn_w], w_out[rnn_w:]
            splits = ((0, rnn_w), (rnn_w, rnn_w), (2 * rnn_w, pool_width))
            wg = _lru_gate_weights(lru_wa[j], lru_wx[j])
            ba = lru_ba[j].reshape(2, 1, rnn_w)
            bx = lru_bx[j].reshape(2, 1, rnn_w)
            lam = lru_lambda[j].reshape(2, 1, rnn_w)
            pool_bd = _block_diag(pool_w[j]).astype(BF16)

            zx_c, zg_c, zp_c = _proj(ctx, gm, sh1c, sc1c, w_in, splits, "ab_in_ctx")
            hf_c, hb_c, h_fin = _lru(zx_c, lru_conv_w[j], lru_conv_b[j], wg, ba, bx, lam,
                                     jnp.zeros((2, B, 1, rnn_w), F32))
            zx, zg, zp = _proj(x, gm, sh1, sc1, w_in, splits, "ab_in")
            hf, hb, _ = _lru(zx, lru_conv_w[j], lru_conv_b[j], wg, ba, bx, lam, h_fin)
            x = _ab_out(x, hf, hb, zg, zp, w_a, w_b, pool_bd, pool_scale[j], g1)
            if need_ctx:
                ctx = _ab_out(ctx, hf_c, hb_c, zg_c, zp_c, w_a, w_b, pool_bd, pool_scale[j], g1c)
        else:
            lambda_init = 0.8 - 0.6 * math.exp(-0.3 * i)
            w_in = cd_w_in[j].astype(BF16)
            w_out = cd_w_out[j].astype(BF16)
            w_c, w_d = w_out[:att_w], w_out[att_w:]
            cos_t, sin_t = _rope_tables(L)
            lq = [a[j].reshape(1, ATT_HD) for a in (diff_lq1, diff_lk1, diff_lq2, diff_lk2)]
            sg = diff_subln_g[j].reshape(ATT_VD, 1)

            if need_ctx:
                raise NotImplementedError("an attention layer followed by further layers is not supported")
            q, k, vt, u = _cd_in(x, gm, sh1, sc1, w_in, cos_t, sin_t, att_w)
            kc, vct = _proj(ctx, gm, sh1c, sc1c, w_in, ((att_w, att_w), (2 * att_w, att_w)), "cd_kv_ctx",
                            transposed=(False, True))
            yc = _attn(q, k, vt, kc, vct, *lq, sg, lambda_init)
            x = _cd_out(x, yc, u, conf_dw_w[j], conf_dw_b[j], conf_ln_g[j], conf_ln_b[j], w_c, w_d, g1)

        x = _ffn(x, gf, sh2, sc2, g2, *ffn_w, final_row, final=(i == depth - 1))
        if need_ctx:
            ctx = _ffn(ctx, gf, sh2c, sc2c, g2c, *ffn_w, final_row, final=False)
    return x
```

```python
import functools
import math

import jax
import jax.numpy as jnp
import numpy as np
from jax import lax
from jax.experimental import pallas as pl
from jax.experimental.pallas import tpu as pltpu

F32 = jnp.float32
BF16 = jnp.bfloat16
SDS = jax.ShapeDtypeStruct

EPS = 1e-6
N_MOD = 6
GRID_W = 64
ROPE_BASE = 10000.0
ROPE_AX = 16
ATT_HD = 64
ATT_VD = 128
LRU_C = 8.0
POOL_WINDOWS = (2, 4, 8, 16)
CONF_K = 31
FFN_CHUNK = 256
FFN_TILE = 1024
ROW_TILE = 1024
LRU_TILE = 512
ATT_TQ = 2048
ATT_TK = 512
LANES = 128
MXU_N = 256
HALO = 16
VMEM_LIMIT = 56 * 1024 * 1024


def _cparams(n_grid):
    return pltpu.CompilerParams(dimension_semantics=("arbitrary",) * n_grid, vmem_limit_bytes=VMEM_LIMIT)


def _tile(L):
    return min(ROW_TILE, L)


def _dot(a, b):
    return jnp.dot(a, b, preferred_element_type=F32)


def _modulate(xf, g, sh, sc):
    ms = jnp.mean(xf * xf, axis=-1, keepdims=True)
    return (xf * lax.rsqrt(ms + EPS) * g) * (1.0 + sc) + sh


def _shift_rows(v, o):
    n = v.shape[0]
    return pltpu.roll(v, (-o) % n, 0)


def _resident(shape):
    nd = len(shape)
    return pl.BlockSpec(shape, lambda *_: (0,) * nd, pipeline_mode=pl.Buffered(1))


def _halo_specs(T, L, width, rows=HALO):
    r = T // rows
    last = L // rows - 1
    prev = pl.BlockSpec((1, rows, width), lambda b, i: (b, jnp.maximum(i * r - 1, 0), 0))
    nxt = pl.BlockSpec((1, rows, width), lambda b, i: (b, jnp.minimum((i + 1) * r, last), 0))
    return prev, nxt


def _mod_body(c_ref, w_ref, b_ref, o_ref):
    s = c_ref[...]
    s = (s * jax.nn.sigmoid(s)).astype(BF16)
    o_ref[0] = _dot(s, w_ref[0].astype(BF16)) + b_ref[0]


def _adaln_all(cv8, mod_w, mod_b):
    depth, D, N = mod_w.shape
    tn = 1024
    return pl.pallas_call(
        _mod_body, grid=(depth, N // tn),
        in_specs=[pl.BlockSpec((8, D), lambda l, j: (0, 0)),
                  pl.BlockSpec((1, D, tn), lambda l, j: (l, 0, j)),
                  pl.BlockSpec((1, 1, tn), lambda l, j: (l, 0, j))],
        out_specs=pl.BlockSpec((1, 8, tn), lambda l, j: (l, 0, j)),
        out_shape=SDS((depth, 8, N), F32),
        compiler_params=_cparams(2), name="adaln",
    )(cv8, mod_w, mod_b.reshape(depth, 1, N))


def _proj_body(x_ref, g_ref, sh_ref, sc_ref, w_ref, *out_refs, splits, transposed):
    h = _modulate(x_ref[0], g_ref[...], sh_ref[0], sc_ref[0]).astype(BF16)
    for (s, wd), tr, o in zip(splits, transposed, out_refs):
        z = _dot(h, w_ref[:, s:s + wd])
        o[0] = (z.T if tr else z).astype(o.dtype)


def _proj(x, g, sh, sc, w, splits, name, transposed=None):
    B, L, D = x.shape
    T = _tile(L)
    transposed = transposed or (False,) * len(splits)
    vec = pl.BlockSpec((1, 1, D), lambda b, i: (b, 0, 0))
    out_specs = [pl.BlockSpec((1, wd, T), lambda b, i: (b, 0, i)) if tr else pl.BlockSpec((1, T, wd), lambda b, i: (b, i, 0))
                 for (_, wd), tr in zip(splits, transposed)]
    out_shape = [SDS((B, wd, L) if tr else (B, L, wd), BF16) for (_, wd), tr in zip(splits, transposed)]
    return pl.pallas_call(
        functools.partial(_proj_body, splits=splits, transposed=transposed), grid=(B, L // T),
        in_specs=[pl.BlockSpec((1, T, D), lambda b, i: (b, i, 0)), _resident((1, D)), vec, vec, _resident(w.shape)],
        out_specs=out_specs, out_shape=out_shape,
        compiler_params=_cparams(2), name=name,
    )(x, g, sh, sc, w)


def _cd_in_body(x_ref, g_ref, sh_ref, sc_ref, w_ref, cos_ref, sin_ref, q_ref, k_ref, vt_ref, u_ref, *, att_w):
    h = _modulate(x_ref[0], g_ref[...], sh_ref[0], sc_ref[0]).astype(BF16)
    cos_t = cos_ref[...]
    sin_t = sin_ref[...]
    low = (lax.broadcasted_iota(jnp.int32, cos_t.shape, 1) % (2 * ROPE_AX)) < ROPE_AX

    def rope(z):
        partner = jnp.where(low, pltpu.roll(z, LANES - ROPE_AX, 1), pltpu.roll(z, ROPE_AX, 1))
        return z * cos_t + partner * sin_t

    q_scale = math.log2(math.e) * ATT_HD ** -0.5
    for c in range(att_w // MXU_N):
        zq = _dot(h, w_ref[:, c * MXU_N:(c + 1) * MXU_N])
        zk = _dot(h, w_ref[:, att_w + c * MXU_N:att_w + (c + 1) * MXU_N])
        for half in range(MXU_N // LANES):
            src = slice(half * LANES, (half + 1) * LANES)
            dst = slice(c * MXU_N + half * LANES, c * MXU_N + (half + 1) * LANES)
            q_ref[0, :, dst] = (rope(zq[:, src]) * q_scale).astype(BF16)
            k_ref[0, :, dst] = rope(zk[:, src]).astype(BF16)
    vt_ref[0] = _dot(h, w_ref[:, 2 * att_w:3 * att_w]).T.astype(BF16)
    u_ref[0] = _dot(h, w_ref[:, 3 * att_w:]).astype(BF16)


def _cd_in(x, g, sh, sc, w, cos_t, sin_t, att_w):
    B, L, D = x.shape
    T = _tile(L)
    conf2 = w.shape[1] - 3 * att_w
    vec = pl.BlockSpec((1, 1, D), lambda b, i: (b, 0, 0))
    tab = pl.BlockSpec((T, LANES), lambda b, i: (i, 0))
    row = lambda wd: pl.BlockSpec((1, T, wd), lambda b, i: (b, i, 0))
    return pl.pallas_call(
        functools.partial(_cd_in_body, att_w=att_w), grid=(B, L // T),
        in_specs=[row(D), _resident((1, D)), vec, vec, _resident(w.shape), tab, tab],
        out_specs=[row(att_w), row(att_w), pl.BlockSpec((1, att_w, T), lambda b, i: (b, 0, i)), row(conf2)],
        out_shape=[SDS((B, L, att_w), BF16), SDS((B, L, att_w), BF16), SDS((B, att_w, L), BF16),
                   SDS((B, L, conf2), BF16)],
        compiler_params=_cparams(2), name="cd_in",
    )(x, g, sh, sc, w, cos_t, sin_t)


def _lru_body(zf_ref, zfp_ref, zfn_ref, zb_ref, zbp_ref, zbn_ref, cw_ref, cb_ref, wg_ref, ba_ref, bx_ref, lam_ref,
              h0_ref, hf_ref, hb_ref, hfin_ref, carry_ref, z_s, xa_s, a_s, b_s, hl_s, al_s, *, T, pad_lo):
    i = pl.program_id(1)
    nt = pl.num_programs(1)
    S = T // 8
    nch = a_s.shape[1]
    K = cw_ref.shape[0]
    base = HALO - pad_lo

    @pl.when(i == 0)
    def _():
        carry_ref[...] = h0_ref[:, 0]

    tiles = ((zf_ref, zfp_ref, zfn_ref, i), (zb_ref, zbp_ref, zbn_ref, nt - 1 - i))
    for d, (z_ref, zp_ref, zn_ref, ti) in enumerate(tiles):
        keep_prev = jnp.where(ti == 0, 0.0, 1.0)
        keep_next = jnp.where(ti == nt - 1, 0.0, 1.0)
        for c in range(nch):
            sl = slice(c * LANES, (c + 1) * LANES)
            z_s[d, c, 0:HALO] = zp_ref[0, :, sl].astype(F32) * keep_prev
            z_s[d, c, HALO:HALO + T] = z_ref[0, :, sl].astype(F32)
            z_s[d, c, HALO + T:] = zn_ref[0, :, sl].astype(F32) * keep_next

            wk = [jnp.broadcast_to(cw_ref[k:k + 1, sl], (8, LANES)) for k in range(K)]
            bias = jnp.broadcast_to(cb_ref[:, sl], (8, LANES))
            win = [z_s[d, c, pl.ds(base + k, 8, stride=S), :] for k in range(K - 1)]
            for j in range(S):
                win.append(z_s[d, c, pl.ds(base + j + K - 1, 8, stride=S), :])
                xa = bias
                for k in range(K):
                    xa = xa + wk[k] * win[k]
                xa_s[d, c, j * 8:(j + 1) * 8, :] = xa
                win.pop(0)

            xc = xa_s[d, c]
            gates = _dot(xc.astype(BF16), wg_ref[d, c])
            r = 0.5 * jnp.tanh(0.5 * (gates[:, :LANES] + ba_ref[d, :, sl])) + 0.5
            gi = 0.5 * jnp.tanh(0.5 * (gates[:, LANES:] + bx_ref[d, :, sl])) + 0.5
            neg_log_a = r * (LRU_C * jax.nn.softplus(-lam_ref[d, :, sl]))
            a = jnp.exp(-neg_log_a)
            one_m_a2 = jnp.tanh(neg_log_a) * (a * a + 1.0)
            root = jnp.where(one_m_a2 > 0.0, one_m_a2 * lax.rsqrt(one_m_a2), 0.0)
            a_s[d, c] = a
            b_s[d, c] = root * (gi * xc)

    def rows_of(d, j):
        return pl.ds(pl.multiple_of((j if d == 0 else S - 1 - j) * 8, 8), 8)

    def p1(j, st):
        hs, cum = st
        nh, ncum = [], []
        for d in range(2):
            rows = rows_of(d, j)
            for c in range(nch):
                a = a_s[d, c, rows, :]
                h = a * hs[d * nch + c] + b_s[d, c, rows, :]
                ac = a * cum[d * nch + c]
                hl_s[d, c, rows, :] = h
                al_s[d, c, rows, :] = ac
                nh.append(h)
                ncum.append(ac)
        return tuple(nh), tuple(ncum)

    zeros = tuple(jnp.zeros((8, LANES), F32) for _ in range(2 * nch))
    ones = tuple(jnp.ones((8, LANES), F32) for _ in range(2 * nch))
    h_end, a_end = lax.fori_loop(0, S, p1, (zeros, ones))

    sub = lax.broadcasted_iota(jnp.int32, (8, LANES), 0)
    cmats = []
    for d in range(2):
        order = range(8) if d == 0 else range(7, -1, -1)
        for c in range(nch):
            sl = slice(c * LANES, (c + 1) * LANES)
            carry = carry_ref[d, :, sl]
            he, ae = h_end[d * nch + c], a_end[d * nch + c]
            cmat = jnp.zeros((8, LANES), F32)
            for s in order:
                cmat = jnp.where(sub == s, carry, cmat)
                carry = ae[s:s + 1, :] * carry + he[s:s + 1, :]
            carry_ref[d, :, sl] = carry
            cmats.append(cmat)
    hfin_ref[:, 0] = carry_ref[...]

    def p2(j, _):
        rows = pl.ds(pl.multiple_of(j * 8, 8), 8)
        for d in range(2):
            for c in range(nch):
                hl_s[d, c, rows, :] = hl_s[d, c, rows, :] + al_s[d, c, rows, :] * cmats[d * nch + c]
        return 0

    lax.fori_loop(0, S, p2, 0)

    out_refs = (hf_ref, hb_ref)
    for d in range(2):
        for c in range(nch):
            for s in range(8):
                out_refs[d][0, s * S:(s + 1) * S, c * LANES:(c + 1) * LANES] = (
                    hl_s[d, c, pl.ds(s, S, stride=8), :].astype(BF16))


def _lru(zx, conv_w, conv_b, wg, ba, bx, lam, h0):
    B, L, W = zx.shape
    T = min(LRU_TILE, L)
    nt = L // T
    nch = W // LANES
    K = conv_w.shape[0]
    fwd = pl.BlockSpec((1, T, W), lambda b, i: (b, i, 0))
    bwd = pl.BlockSpec((1, T, W), lambda b, i: (b, nt - 1 - i, 0))
    r = T // HALO
    last = L // HALO - 1
    fprev, fnext = _halo_specs(T, L, W)
    bprev = pl.BlockSpec((1, HALO, W), lambda b, i: (b, jnp.maximum((nt - 1 - i) * r - 1, 0), 0))
    bnext = pl.BlockSpec((1, HALO, W), lambda b, i: (b, jnp.minimum((nt - i) * r, last), 0))
    state = pl.BlockSpec((2, 1, 1, W), lambda b, i: (0, b, 0, 0))
    scr = pltpu.VMEM((2, nch, T, LANES), F32)
    return pl.pallas_call(
        functools.partial(_lru_body, T=T, pad_lo=K // 2), grid=(B, nt),
        in_specs=[fwd, fprev, fnext, bwd, bprev, bnext, _resident(conv_w.shape), _resident((1, W)),
                  _resident(wg.shape), _resident(ba.shape), _resident(bx.shape), _resident(lam.shape), state],
        out_specs=[fwd, bwd, state],
        out_shape=[SDS((B, L, W), BF16), SDS((B, L, W), BF16), SDS((2, B, 1, W), F32)],
        scratch_shapes=[pltpu.VMEM((2, 1, W), F32), pltpu.VMEM((2, nch, T + 2 * HALO, LANES), F32),
                        scr, scr, scr, scr, scr],
        compiler_params=_cparams(2), name="lru",
    )(zx, zx, zx, zx, zx, zx, conv_w, conv_b.reshape(1, W), wg, ba, bx, lam, h0)


def _ab_out_body(x_ref, hf_ref, hb_ref, zg_ref, zp_ref, zpp_ref, zpn_ref, wa_ref, wb_ref, pbd_ref, ps_ref, g1_ref,
                 o_ref, *, T, L):
    i = pl.program_id(1)
    nt = pl.num_programs(1)
    ya = (hf_ref[0].astype(F32) + hb_ref[0].astype(F32)) * jax.nn.gelu(zg_ref[0].astype(F32))
    y = _dot(ya.astype(BF16), wa_ref[...])

    keep_prev = jnp.where(i == 0, 0.0, 1.0)
    keep_next = jnp.where(i == nt - 1, 0.0, 1.0)
    xc = zp_ref[0].astype(F32)
    xe = jnp.concatenate([zpp_ref[0].astype(F32) * keep_prev, xc, zpn_ref[0].astype(F32) * keep_next], axis=0)
    sums = [xe + _shift_rows(xe, -1)]
    for half in (1, 2, 4):
        sums.append(_shift_rows(sums[-1], -half) + _shift_rows(sums[-1], half))
    lane = lax.broadcasted_iota(jnp.int32, xc.shape, 1)
    tpos = i * T + lax.broadcasted_iota(jnp.int32, xc.shape, 0)
    gd = xc.shape[1] // len(POOL_WINDOWS)
    mean = None
    for gi in range(len(POOL_WINDOWS) - 1, -1, -1):
        w = POOL_WINDOWS[gi]
        cnt = (jnp.minimum(tpos + (w - w // 2), L) - jnp.maximum(tpos - w // 2, 0)).astype(F32)
        m = sums[gi][HALO:HALO + T] / cnt
        mean = m if mean is None else jnp.where(lane < (gi + 1) * gd, m, mean)
    yb = _dot((mean - xc).astype(BF16), pbd_ref[...]) * ps_ref[...]
    y = y + _dot(yb.astype(BF16), wb_ref[...])
    o_ref[0] = x_ref[0] + g1_ref[0] * y


def _ab_out(x, hf, hb, zg, zp, w_a, w_b, pool_bd, pool_scale, g1):
    B, L, D = x.shape
    T = _tile(L)
    RW, PW = hf.shape[2], zp.shape[2]
    row = lambda wd: pl.BlockSpec((1, T, wd), lambda b, i: (b, i, 0))
    pprev, pnext = _halo_specs(T, L, PW)
    return pl.pallas_call(
        functools.partial(_ab_out_body, T=T, L=L), grid=(B, L // T),
        in_specs=[row(D), row(RW), row(RW), row(RW), row(PW), pprev, pnext, _resident(w_a.shape),
                  _resident(w_b.shape), _resident(pool_bd.shape), _resident((1, PW)),
                  pl.BlockSpec((1, 1, D), lambda b, i: (b, 0, 0))],
        out_specs=row(D), out_shape=SDS((B, L, D), F32),
        compiler_params=_cparams(2), name="ab_out",
    )(x, hf, hb, zg, zp, zp, zp, w_a, w_b, pool_bd, pool_scale.reshape(1, PW), g1)


def _ffn_body(x_ref, xp_ref, xn_ref, g_ref, sh_ref, sc_ref, g2_ref, wu_ref, cw_ref, cb_ref, wd_ref, fg_ref, o_ref,
              acc_ref, hext_s, ua_s, ub_s, *, T, final):
    i = pl.program_id(1)
    nt = pl.num_programs(1)
    ff = wd_ref.shape[0]
    nf = ff // FFN_CHUNK

    def lin_cols(f):
        return pl.ds(pl.multiple_of(f * FFN_CHUNK, FFN_CHUNK), FFN_CHUNK)

    def gate_cols(f):
        return pl.ds(pl.multiple_of(ff + f * FFN_CHUNK, FFN_CHUNK), FFN_CHUNK)
    g, sh, sc = g_ref[...], sh_ref[0], sc_ref[0]
    hp = _modulate(xp_ref[0], g, sh, sc) * jnp.where(i == 0, 0.0, 1.0)
    hn = _modulate(xn_ref[0], g, sh, sc) * jnp.where(i == nt - 1, 0.0, 1.0)
    hext_s[...] = jnp.concatenate([hp, _modulate(x_ref[0], g, sh, sc), hn], axis=0).astype(BF16)
    H = xp_ref.shape[1]

    def conv(u, cw, cb):
        return (cw[0:1] * _shift_rows(u, -1) + cw[1:2] * u + cw[2:3] * _shift_rows(u, 1))[H:H + T] + cb

    def up(f, u_s):
        u_s[0] = _dot(hext_s[...], wu_ref[:, lin_cols(f)])
        u_s[1] = _dot(hext_s[...], wu_ref[:, gate_cols(f)])

    def down(f, u_s):
        lin = conv(u_s[0], cw_ref[:, lin_cols(f)], cb_ref[:, lin_cols(f)])
        gate = conv(u_s[1], cw_ref[:, gate_cols(f)], cb_ref[:, gate_cols(f)])
        act = (gate * jax.nn.sigmoid(gate) * lin).astype(BF16)
        acc_ref[...] += _dot(act, wd_ref[lin_cols(f), :])

    acc_ref[...] = jnp.zeros_like(acc_ref)
    up(0, ua_s)

    def pair(jj, _):
        up(2 * jj + 1, ub_s)
        down(2 * jj, ua_s)
        up(2 * jj + 2, ua_s)
        down(2 * jj + 1, ub_s)
        return 0

    lax.fori_loop(0, (nf - 1) // 2, pair, 0)
    if nf % 2 == 0:
        up(nf - 1, ub_s)
        down(nf - 2, ua_s)
        down(nf - 1, ub_s)
    else:
        down(nf - 1, ua_s)
    out = x_ref[0] + g2_ref[0] * acc_ref[...]
    if final:
        out = out * lax.rsqrt(jnp.mean(out * out, axis=-1, keepdims=True) + EPS) * fg_ref[...]
    o_ref[0] = out


def _ffn(x, g, sh, sc, g2, wu, cw, cb, wd, final_g, final, tile=FFN_TILE):
    B, L, D = x.shape
    T = min(tile, L)
    H = 8
    assert wd.shape[0] % FFN_CHUNK == 0
    vec = pl.BlockSpec((1, 1, D), lambda b, i: (b, 0, 0))
    xprev, xnext = _halo_specs(T, L, D, rows=H)
    row = pl.BlockSpec((1, T, D), lambda b, i: (b, i, 0))
    return pl.pallas_call(
        functools.partial(_ffn_body, T=T, final=final), grid=(B, L // T),
        in_specs=[row, xprev, xnext, _resident((1, D)), vec, vec, vec, _resident(wu.shape), _resident(cw.shape),
                  _resident(cb.shape), _resident(wd.shape), _resident((1, D))],
        out_specs=row, out_shape=SDS((B, L, D), F32),
        scratch_shapes=[pltpu.VMEM((T, D), F32), pltpu.VMEM((T + 2 * H, D), BF16),
                        pltpu.VMEM((2, T + 2 * H, FFN_CHUNK), F32), pltpu.VMEM((2, T + 2 * H, FFN_CHUNK), F32)],
        compiler_params=_cparams(2), name="ffn",
    )(x, x, x, g, sh, sc, g2, wu, cw, cb, wd, final_g)


def _attn_body(q_ref, k_ref, vt_ref, kc_ref, vct_ref, lq1_ref, lk1_ref, lq2_ref, lk2_ref, sg_ref, o_ref,
               qm_s, sa_s, sb_s, sc_s, m_s, l_s, acc_s, *, Tk, lambda_init):
    Tq = q_ref.shape[1]
    nk = k_ref.shape[1] // Tk
    nt_dims = (((1,), (1,)), ((), ()))
    cols = [slice(c * MXU_N, (c + 1) * MXU_N) for c in range(Tq // MXU_N)]

    q = q_ref[0]
    lane = lax.broadcasted_iota(jnp.int32, q.shape, 1)
    zero = jnp.zeros_like(q)
    qm_s[0] = jnp.where(lane < ATT_HD, q, zero)
    qm_s[1] = jnp.where(lane >= ATT_HD, q, zero)
    m_s[...] = jnp.full(m_s.shape, -jnp.inf, F32)
    l_s[...] = jnp.zeros(l_s.shape, F32)
    acc_s[...] = jnp.zeros(acc_s.shape, F32)

    def scores(kb, s_ref):
        for comp in range(2):
            for cs in cols:
                s_ref[comp, :, cs] = lax.dot_general(kb, qm_s[comp, cs, :], nt_dims, preferred_element_type=F32)

    def consume(s_ref, vtb):
        for comp in range(2):
            for cs in cols:
                s = s_ref[comp, :, cs]
                m_old = m_s[comp, :, cs]
                m_new = jnp.maximum(m_old, jnp.max(s, axis=0, keepdims=True))
                alpha = jnp.exp2(m_old - m_new)
                p = jnp.exp2(s - m_new)
                l_s[comp, :, cs] = alpha * l_s[comp, :, cs] + jnp.sum(p, axis=0, keepdims=True)
                acc_s[comp, :, cs] = alpha * acc_s[comp, :, cs] + _dot(vtb, p.astype(BF16))
                m_s[comp, :, cs] = m_new

    def kblock(j):
        return k_ref[0, pl.ds(pl.multiple_of(j * Tk, Tk), Tk), :]

    def vblock(j):
        return vt_ref[0, :, pl.ds(pl.multiple_of(j * Tk, Tk), Tk)]

    scores(kc_ref[0], sc_s)
    scores(kblock(0), sa_s)
    consume(sc_s, vct_ref[0])

    def pair(jj, _):
        scores(kblock(2 * jj + 1), sb_s)
        consume(sa_s, vblock(2 * jj))
        scores(kblock(2 * jj + 2), sa_s)
        consume(sb_s, vblock(2 * jj + 1))
        return 0

    lax.fori_loop(0, nk // 2 - 1, pair, 0)
    scores(kblock(nk - 1), sb_s)
    consume(sa_s, vblock(nk - 2))
    consume(sb_s, vblock(nk - 1))

    lam = (jnp.exp(jnp.sum(lq1_ref[...] * lk1_ref[...], axis=-1, keepdims=True))
           - jnp.exp(jnp.sum(lq2_ref[...] * lk2_ref[...], axis=-1, keepdims=True)) + lambda_init)
    o = acc_s[0] * (1.0 / l_s[0]) - acc_s[1] * (lam / l_s[1])
    y = o * lax.rsqrt(jnp.mean(o * o, axis=0, keepdims=True) + EPS) * (sg_ref[...] * (1.0 - lambda_init))
    o_ref[0] = y.T.astype(o_ref.dtype)


def _attn(q, k, vt, kc, vct, lq1, lk1, lq2, lk2, subln_g, lambda_init, Tq=ATT_TQ, Tk=ATT_TK):
    B, L, W = q.shape
    Lc = kc.shape[1]
    H = W // ATT_VD
    Tq, Tk = min(Tq, L), min(Tk, L)
    qspec = pl.BlockSpec((1, Tq, ATT_VD), lambda b, h, i: (b, i, h))
    kspec = pl.BlockSpec((1, L, ATT_VD), lambda b, h, i: (b, 0, h))
    vspec = pl.BlockSpec((1, ATT_VD, L), lambda b, h, i: (b, h, 0))
    kcspec = pl.BlockSpec((1, Lc, ATT_VD), lambda b, h, i: (b, 0, h))
    vcspec = pl.BlockSpec((1, ATT_VD, Lc), lambda b, h, i: (b, h, 0))
    small = _resident((1, ATT_HD))
    assert (L // Tk) % 2 == 0 and Tq % MXU_N == 0
    score_slot = pltpu.VMEM((2, Tk, Tq), F32)
    return pl.pallas_call(
        functools.partial(_attn_body, Tk=Tk, lambda_init=lambda_init), grid=(B, H, L // Tq),
        in_specs=[qspec, kspec, vspec, kcspec, vcspec, small, small, small, small, _resident((ATT_VD, 1))],
        out_specs=qspec, out_shape=SDS((B, L, W), BF16),
        scratch_shapes=[pltpu.VMEM((2, Tq, ATT_VD), BF16), score_slot, score_slot, pltpu.VMEM((2, Lc, Tq), F32),
                        pltpu.VMEM((2, 1, Tq), F32), pltpu.VMEM((2, 1, Tq), F32), pltpu.VMEM((2, ATT_VD, Tq), F32)],
        compiler_params=_cparams(3), name="diff_attn",
    )(q, k, vt, kc, vct, lq1, lk1, lq2, lk2, subln_g)


def _cd_out_body(x_ref, yc_ref, u_ref, up_ref, un_ref, dww_ref, dwb_ref, lng_ref, lnb_ref, wc_ref, wd_ref, g1_ref,
                 o_ref, *, T):
    i = pl.program_id(1)
    nt = pl.num_programs(1)
    CW = dww_ref.shape[1]
    keep_prev = jnp.where(i == 0, 0.0, 1.0)
    keep_next = jnp.where(i == nt - 1, 0.0, 1.0)
    ue = jnp.concatenate([up_ref[0].astype(F32) * keep_prev, u_ref[0].astype(F32),
                          un_ref[0].astype(F32) * keep_next], axis=0)
    ge = ue[:, :CW] * jax.nn.sigmoid(ue[:, CW:])
    base = HALO - CONF_K // 2
    conv = dwb_ref[...]
    for r in range(8):
        shifted = _shift_rows(ge, r) if r else ge
        for k in range(CONF_K):
            off = base + k
            if off % 8 == r:
                conv = conv + dww_ref[k:k + 1, :] * shifted[off - r:off - r + T]
    mu = jnp.mean(conv, axis=-1, keepdims=True)
    cen = conv - mu
    var = jnp.mean(cen * cen, axis=-1, keepdims=True)
    ln = cen * lax.rsqrt(var + EPS) * lng_ref[...] + lnb_ref[...]
    yd = ln * jax.nn.sigmoid(ln)
    y = _dot(yc_ref[0], wc_ref[...]) + _dot(yd.astype(BF16), wd_ref[...])
    o_ref[0] = x_ref[0] + g1_ref[0] * y


def _cd_out(x, yc, u, dw_w, dw_b, ln_g, ln_b, w_c, w_d, g1):
    B, L, D = x.shape
    T = _tile(L)
    AW, UW, CW = yc.shape[2], u.shape[2], dw_w.shape[1]
    row = lambda wd: pl.BlockSpec((1, T, wd), lambda b, i: (b, i, 0))
    uprev, unext = _halo_specs(T, L, UW)
    return pl.pallas_call(
        functools.partial(_cd_out_body, T=T), grid=(B, L // T),
        in_specs=[row(D), row(AW), row(UW), uprev, unext, _resident(dw_w.shape), _resident((1, CW)),
                  _resident((1, CW)), _resident((1, CW)), _resident(w_c.shape), _resident(w_d.shape),
                  pl.BlockSpec((1, 1, D), lambda b, i: (b, 0, 0))],
        out_specs=row(D), out_shape=SDS((B, L, D), F32),
        compiler_params=_cparams(2), name="cd_out",
    )(x, yc, u, u, u, dw_w, dw_b.reshape(1, CW), ln_g.reshape(1, CW), ln_b.reshape(1, CW), w_c, w_d, g1)


def _block_diag(blocks):
    n, r, c = blocks.shape
    eye = jnp.eye(n, dtype=blocks.dtype)
    return (eye[:, None, :, None] * blocks[:, :, None, :]).reshape(n * r, n * c)


def _lru_gate_weights(wa, wx):
    ndir, heads, hd, _ = wa.shape
    per = LANES // hd

    def chunked(w):
        w = w.reshape(ndir * heads // per, per, hd, hd)
        return jax.vmap(_block_diag)(w).reshape(ndir, heads // per, LANES, LANES)

    return jnp.concatenate([chunked(wa), chunked(wx)], axis=-1).astype(BF16)


def _rope_tables(L):
    rows = L // GRID_W
    row = np.repeat(np.arange(rows), GRID_W).astype(np.float32)
    col = np.tile(np.arange(GRID_W), rows).astype(np.float32)
    inv = (ROPE_BASE ** (-np.arange(ROPE_AX, dtype=np.float32) / ROPE_AX)).astype(np.float32)
    ang_r = row[:, None] * inv
    ang_c = col[:, None] * inv
    ang = np.concatenate([ang_r, ang_r, ang_c, ang_c] * (LANES // (4 * ROPE_AX)), axis=-1)
    sign = np.tile(np.concatenate([-np.ones(ROPE_AX, np.float32), np.ones(ROPE_AX, np.float32)]), LANES // (2 * ROPE_AX))
    return jnp.asarray(np.cos(ang), F32), jnp.asarray(np.sin(ang) * sign, F32)


def _ffn_weights(w_up, conv_w, conv_b, w_down):
    return w_up.astype(BF16), conv_w, conv_b.reshape(1, -1), w_down.astype(BF16)


def kernel(x, c, ctx, c_ctx, mod_w, mod_b, norm_mix_g, norm_ffn_g, ab_w_in, ab_w_out, lru_conv_w, lru_conv_b, lru_wa, lru_ba, lru_wx, lru_bx, lru_lambda, pool_w, pool_scale, cd_w_in, cd_w_out, diff_lq1, diff_lk1, diff_lq2, diff_lk2, diff_subln_g, conf_dw_w, conf_dw_b, conf_ln_g, conf_ln_b, ffn_w_up, ffn_conv_w, ffn_conv_b, ffn_w_down, final_g):
    B, L, D = x.shape
    depth = mod_w.shape[0]
    rnn_w = lru_conv_w.shape[-1]
    pool_width = pool_scale.shape[-1]
    att_w = cd_w_out.shape[1] - conf_dw_w.shape[-1]

    cv8 = jnp.zeros((8, D), F32).at[:B].set(c).at[B].set(c_ctx)
    mods = _adaln_all(cv8, mod_w, mod_b)
    final_row = final_g.reshape(1, D)

    for i in range(depth):
        j = i // 2
        need_ctx = i < depth - 1
        m = mods[i].reshape(8, N_MOD, D)
        sh1, sc1, g1, sh2, sc2, g2 = [m[:B, n][:, None, :] for n in range(N_MOD)]
        sh1c, sc1c, g1c, sh2c, sc2c, g2c = [jnp.broadcast_to(m[B, n][None, None, :], (B, 1, D)) for n in range(N_MOD)]
        gm = norm_mix_g[i].reshape(1, D)
        gf = norm_ffn_g[i].reshape(1, D)
        ffn_w = _ffn_weights(ffn_w_up[i], ffn_conv_w[i], ffn_conv_b[i], ffn_w_down[i])

        if i % 2 == 0:
            w_in = ab_w_in[j].astype(BF16)
            w_out = ab_w_out[j].astype(BF16)
            w_a, w_b = w_out[:rnn_w], w_out[rnn_w:]
            splits = ((0, rnn_w), (rnn_w, rnn_w), (2 * rnn_w, pool_width))
            wg = _lru_gate_weights(lru_wa[j], lru_wx[j])
            ba = lru_ba[j].reshape(2, 1, rnn_w)
            bx = lru_bx[j].reshape(2, 1, rnn_w)
            lam = lru_lambda[j].reshape(2, 1, rnn_w)
            pool_bd = _block_diag(pool_w[j]).astype(BF16)

            zx_c, zg_c, zp_c = _proj(ctx, gm, sh1c, sc1c, w_in, splits, "ab_in_ctx")
            hf_c, hb_c, h_fin = _lru(zx_c, lru_conv_w[j], lru_conv_b[j], wg, ba, bx, lam,
                                     jnp.zeros((2, B, 1, rnn_w), F32))
            zx, zg, zp = _proj(x, gm, sh1, sc1, w_in, splits, "ab_in")
            hf, hb, _ = _lru(zx, lru_conv_w[j], lru_conv_b[j], wg, ba, bx, lam, h_fin)
            x = _ab_out(x, hf, hb, zg, zp, w_a, w_b, pool_bd, pool_scale[j], g1)
            if need_ctx:
                ctx = _ab_out(ctx, hf_c, hb_c, zg_c, zp_c, w_a, w_b, pool_bd, pool_scale[j], g1c)
        else:
            lambda_init = 0.8 - 0.6 * math.exp(-0.3 * i)
            w_in = cd_w_in[j].astype(BF16)
            w_out = cd_w_out[j].astype(BF16)
            w_c, w_d = w_out[:att_w], w_out[att_w:]
            cos_t, sin_t = _rope_tables(L)
            lq = [a[j].reshape(1, ATT_HD) for a in (diff_lq1, diff_lk1, diff_lq2, diff_lk2)]
            sg = diff_subln_g[j].reshape(ATT_VD, 1)

            if need_ctx:
                raise NotImplementedError("an attention layer followed by further layers is not supported")
            q, k, vt, u = _cd_in(x, gm, sh1, sc1, w_in, cos_t, sin_t, att_w)
            kc, vct = _proj(ctx, gm, sh1c, sc1c, w_in, ((att_w, att_w), (2 * att_w, att_w)), "cd_kv_ctx",
                            transposed=(False, True))
            yc = _attn(q, k, vt, kc, vct, *lq, sg, lambda_init)
            x = _cd_out(x, yc, u, conf_dw_w[j], conf_dw_b[j], conf_ln_g[j], conf_ln_b[j], w_c, w_d, g1)

        x = _ffn(x, gf, sh2, sc2, g2, *ffn_w, final_row, final=(i == depth - 1))
        if need_ctx:
            ctx = _ffn(ctx, gf, sh2c, sc2c, g2c, *ffn_w, final_row, final=False)
    return x
```
